```python
import math
import jax, jax.numpy as jnp
from jax import lax
import numpy as np

D_MODEL = 4096
BATCH = 2
SEQ = 8192
DEPTH = 1

HEAD_DIM = 128
DN_HEADS = 16
DN_DK = HEAD_DIM
DN_DV = HEAD_DIM
DN_QK = DN_HEADS * DN_DK
DN_WIDTH = DN_HEADS * DN_DV
CONV_K = 4
CHUNK = 64
DF_HEADS = 8
DF_DH = HEAD_DIM
DF_DV = 2 * DF_DH
DF_QK = 2 * DF_HEADS * DF_DH
DF_WIDTH = DF_HEADS * DF_DV
MIX_WIDTH = DN_WIDTH + DF_WIDTH
ROPE_THETA = 500000.0
ROPE_DIM = DF_DH // 4
Q_BLOCK = 128
FFN_HIDDEN = -(-8 * D_MODEL // (3 * 256)) * 256
EPS = 1e-6
PROJ_SIZES = (DN_QK, DN_QK, DN_WIDTH, DN_WIDTH, DN_HEADS, DN_HEADS, DF_QK, DF_QK, DF_WIDTH)
PROJ_TOTAL = sum(PROJ_SIZES)
PROJ_SPLIT = tuple(int(v) for v in np.cumsum(PROJ_SIZES)[:-1])
CONV_CH = 2 * DN_QK + DN_WIDTH

kernel_name = "hybrid_gdn_diffattn_parallel_heads"


def rmsnorm(x, w):
    xf = x.astype(jnp.float32)
    xf = xf * lax.rsqrt(jnp.mean(xf * xf, axis=-1, keepdims=True) + EPS)
    return (xf * w.astype(jnp.float32)).astype(x.dtype)


def l2norm(x):
    return x * lax.rsqrt(jnp.sum(x * x, axis=-1, keepdims=True) + EPS)


def causal_depthwise_conv(x, w):
    C = x.shape[-1]
    return lax.conv_general_dilated(
        x, w.astype(x.dtype)[:, None, :], window_strides=(1,),
        padding=[(w.shape[0] - 1, 0)],
        dimension_numbers=("NWC", "WIO", "NWC"), feature_group_count=C)


def rotary_tables(S):
    pos = jnp.arange(S, dtype=jnp.float32)
    inv = ROPE_THETA ** (-jnp.arange(0, ROPE_DIM, 2, dtype=jnp.float32) / ROPE_DIM)
    ang = pos[:, None] * inv[None, :]
    return jnp.cos(ang), jnp.sin(ang)


def partial_rotary(x, cos, sin):
    half = ROPE_DIM // 2
    xr = x[..., :ROPE_DIM].astype(jnp.float32)
    x1, x2 = xr[..., :half], xr[..., half:]
    c, s = cos[None, :, None, :], sin[None, :, None, :]
    rot = jnp.concatenate([x1 * c - x2 * s, x2 * c + x1 * s], axis=-1)
    return jnp.concatenate([rot.astype(x.dtype), x[..., ROPE_DIM:]], axis=-1)


def gated_delta_rule(q, k, v, g, beta):
    B, S, H, dk = q.shape
    dv = v.shape[-1]
    N = S // CHUNK
    q = l2norm(q.astype(jnp.float32)) * (dk ** -0.5)
    k = l2norm(k.astype(jnp.float32))
    v = v.astype(jnp.float32)

    def to_chunks(t):
        return t.reshape((B, N, CHUNK, H) + t.shape[3:]).transpose((0, 3, 1, 2) + tuple(range(4, t.ndim + 1)))

    q, k, v = to_chunks(q), to_chunks(k), to_chunks(v)
    g, beta = to_chunks(g), to_chunks(beta)
    gc = jnp.cumsum(g, axis=-1)
    tril = jnp.tril(jnp.ones((CHUNK, CHUNK), dtype=bool))
    strict = jnp.tril(jnp.ones((CHUNK, CHUNK), dtype=bool), k=-1)
    decay = jnp.exp(jnp.where(tril, gc[..., :, None] - gc[..., None, :], -jnp.inf))
    kb = k * beta[..., None]
    vb = v * beta[..., None]
    A = jnp.where(strict, jnp.einsum('bhnid,bhnjd->bhnij', kb, k) * decay, 0.0)
    eye = jnp.eye(CHUNK, dtype=jnp.float32)
    rhs = jnp.concatenate([vb, kb * jnp.exp(gc)[..., None]], axis=-1)
    sol = lax.linalg.triangular_solve(eye + A, rhs, left_side=True, lower=True)
    u, w = sol[..., :dv], sol[..., dv:]
    attn = jnp.einsum('bhnid,bhnjd->bhnij', q, k) * decay

    def step(state, inp):
        qc, kc, uc, wc, gcc, ac = inp
        v_new = uc - jnp.einsum('bhck,bhkv->bhcv', wc, state)
        o = (jnp.einsum('bhck,bhkv->bhcv', qc * jnp.exp(gcc)[..., None], state)
             + jnp.einsum('bhij,bhjv->bhiv', ac, v_new))
        g_last = gcc[..., -1]
        state = (state * jnp.exp(g_last)[..., None, None]
                 + jnp.einsum('bhck,bhcv->bhkv', kc * jnp.exp(g_last[..., None] - gcc)[..., None], v_new))
        return state, o

    xs = tuple(jnp.moveaxis(t, 2, 0) for t in (q, k, u, w, gc, attn))
    state0 = jnp.zeros((B, H, dk, dv), jnp.float32)
    _, o = lax.scan(step, state0, xs)
    return o.transpose(1, 0, 3, 2, 4).reshape(B, S, H, dv)


def diff_attention(q, k, v, lam):
    B, S, H2, dh = q.shape
    H = H2 // 2
    nq = S // Q_BLOCK
    scale = dh ** -0.5
    qb = q.reshape(B, nq, Q_BLOCK, H2, dh).transpose(1, 0, 2, 3, 4)
    kf = k.astype(jnp.float32)
    vf = v.astype(jnp.float32)
    kpos = jnp.arange(S)

    def block(args):
        q_blk, idx = args
        s = jnp.einsum('bqhd,bkhd->bhqk', q_blk.astype(jnp.float32), kf) * scale
        qpos = idx * Q_BLOCK + jnp.arange(Q_BLOCK)
        s = jnp.where(kpos[None, :] <= qpos[:, None], s, -jnp.inf)
        p = jax.nn.softmax(s, axis=-1).reshape(B, H, 2, Q_BLOCK, S)
        a = p[:, :, 0] - lam * p[:, :, 1]
        return jnp.einsum('bhqk,bkhe->bqhe', a, vf)

    o = lax.map(block, (qb, jnp.arange(nq)))
    return o.transpose(1, 0, 2, 3, 4).reshape(B, S, H, 2 * dh)


def setup_inputs(seed: int = 0) -> dict:
    key = jax.random.key(seed)
    ks = jax.random.split(key, 20)
    f32 = jnp.float32
    L = DEPTH

    def nrm(k, shape, scale):
        return jax.random.normal(k, shape, f32) * scale

    def gain(k, n):
        return 1.0 + 0.02 * jax.random.normal(k, (L, n), f32)

    dt = jnp.exp(jax.random.uniform(ks[5], (L, DN_HEADS), f32, math.log(1e-3), math.log(1e-1)))
    return {
        "x": jax.random.normal(ks[0], (BATCH, SEQ, D_MODEL), f32),
        "ln_mix_w": gain(ks[1], D_MODEL),
        "w_in": nrm(ks[2], (L, D_MODEL, PROJ_TOTAL), D_MODEL ** -0.5),
        "conv_w": nrm(ks[3], (L, CONV_K, CONV_CH), CONV_K ** -0.5),
        "a_log": jnp.log(jax.random.uniform(ks[4], (L, DN_HEADS), f32, 1.0, 16.0)),
        "dt_bias": dt + jnp.log(-jnp.expm1(-dt)),
        "dn_norm_w": gain(ks[6], DN_DV),
        "lambda_q1": nrm(ks[7], (L, DF_DH), 0.1),
        "lambda_k1": nrm(ks[8], (L, DF_DH), 0.1),
        "lambda_q2": nrm(ks[9], (L, DF_DH), 0.1),
        "lambda_k2": nrm(ks[10], (L, DF_DH), 0.1),
        "df_norm_w": gain(ks[11], DF_DV),
        "w_out": nrm(ks[12], (L, MIX_WIDTH, D_MODEL), MIX_WIDTH ** -0.5),
        "ln_ffn_w": gain(ks[13], D_MODEL),
        "w_gate": nrm(ks[14], (L, D_MODEL, FFN_HIDDEN), D_MODEL ** -0.5),
        "w_up": nrm(ks[15], (L, D_MODEL, FFN_HIDDEN), D_MODEL ** -0.5),
        "w_down": nrm(ks[16], (L, FFN_HIDDEN, D_MODEL), FFN_HIDDEN ** -0.5),
        "ln_final_w": 1.0 + 0.02 * jax.random.normal(ks[17], (D_MODEL,), f32),
    }


def reference(x, ln_mix_w, w_in, conv_w, a_log, dt_bias, dn_norm_w,
              lambda_q1, lambda_k1, lambda_q2, lambda_k2, df_norm_w, w_out,
              ln_ffn_w, w_gate, w_up, w_down, ln_final_w):
    B, S, _ = x.shape
    cos, sin = rotary_tables(S)
    for l in range(DEPTH):
        h = rmsnorm(x, ln_mix_w[l])
        proj = h @ w_in[l]
        dq, dk, dv, dz, db, da, fq, fk, fv = jnp.split(proj, PROJ_SPLIT, axis=-1)

        qkv = jax.nn.silu(causal_depthwise_conv(jnp.concatenate([dq, dk, dv], axis=-1), conv_w[l]))
        dq, dk, dv = jnp.split(qkv, (DN_QK, 2 * DN_QK), axis=-1)
        beta = jax.nn.sigmoid(db.astype(jnp.float32))
        g = -jnp.exp(a_log[l].astype(jnp.float32)) * jax.nn.softplus(
            da.astype(jnp.float32) + dt_bias[l].astype(jnp.float32))
        o_dn = gated_delta_rule(dq.reshape(B, S, DN_HEADS, DN_DK), dk.reshape(B, S, DN_HEADS, DN_DK),
                                dv.reshape(B, S, DN_HEADS, DN_DV), g, beta)
        o_dn = rmsnorm(o_dn, dn_norm_w[l]) * jax.nn.silu(dz.reshape(B, S, DN_HEADS, DN_DV).astype(jnp.float32))

        lambda_init = 0.8 - 0.6 * math.exp(-0.3 * l)
        lam = (jnp.exp(jnp.sum(lambda_q1[l].astype(jnp.float32) * lambda_k1[l].astype(jnp.float32)))
               - jnp.exp(jnp.sum(lambda_q2[l].astype(jnp.float32) * lambda_k2[l].astype(jnp.float32)))
               + lambda_init)
        fq = partial_rotary(fq.reshape(B, S, 2 * DF_HEADS, DF_DH), cos, sin)
        fk = partial_rotary(fk.reshape(B, S, 2 * DF_HEADS, DF_DH), cos, sin)
        o_df = diff_attention(fq, fk, fv.reshape(B, S, DF_HEADS, DF_DV), lam)
        o_df = rmsnorm(o_df, df_norm_w[l]) * (1.0 - lambda_init)

        mix = jnp.concatenate([o_dn.reshape(B, S, DN_WIDTH), o_df.reshape(B, S, DF_WIDTH)],
                              axis=-1).astype(x.dtype)
        x = x + mix @ w_out[l]

        h = rmsnorm(x, ln_ffn_w[l])
        x = x + (jax.nn.silu(h @ w_gate[l]) * (h @ w_up[l])) @ w_down[l]
    return rmsnorm(x, ln_final_w)
```

```python
import functools
import math

import jax
import jax.numpy as jnp
import numpy as np
from jax import lax
from jax.experimental import pallas as pl
from jax.experimental.pallas import tpu as pltpu

F32 = jnp.float32
BF16 = jnp.bfloat16

EPS = 1e-6
HEAD_DIM = 128
DN_HEADS = 16
DN_WIDTH = DN_HEADS * HEAD_DIM
CONV_K = 4
CHUNK = 64
DF_HEADS = 8
DF_DV = 2 * HEAD_DIM
DF_WIDTH = DF_HEADS * DF_DV
ROPE_THETA = 500000.0
ROPE_DIM = HEAD_DIM // 4
LANES = 128
VMEM_LIMIT = 56 * 1024 * 1024


def _cparams(*sem):
    return pltpu.CompilerParams(dimension_semantics=sem, vmem_limit_bytes=VMEM_LIMIT)


def _silu(x):
    return x / (1.0 + jnp.exp(-x))


def _inproj_body(x_ref, lnw_ref, w_ref, wba_ref, o_ref, oba_ref, h_ref):
    @pl.when(pl.program_id(1) == 0)
    def _():
        x = x_ref[...]
        ms = jnp.mean(x * x, axis=-1, keepdims=True)
        h = (x * lax.rsqrt(ms + EPS) * lnw_ref[...]).astype(BF16)
        h_ref[...] = h
        oba_ref[...] = jnp.dot(h, wba_ref[...], preferred_element_type=F32)

    o_ref[...] = jnp.dot(h_ref[...], w_ref[...], preferred_element_type=F32).astype(o_ref.dtype)


def _inproj(x2d, lnw, w_main, w_ba, tm, tn):
    T, D = x2d.shape
    N = w_main.shape[1]
    return pl.pallas_call(
        _inproj_body,
        grid=(T // tm, N // tn),
        in_specs=[
            pl.BlockSpec((tm, D), lambda i, j: (i, 0)),
            pl.BlockSpec((1, D), lambda i, j: (0, 0)),
            pl.BlockSpec((D, tn), lambda i, j: (0, j)),
            pl.BlockSpec((D, LANES), lambda i, j: (0, 0)),
        ],
        out_specs=[
            pl.BlockSpec((tm, tn), lambda i, j: (i, j)),
            pl.BlockSpec((tm, LANES), lambda i, j: (i, 0)),
        ],
        out_shape=[
            jax.ShapeDtypeStruct((T, N), F32),
            jax.ShapeDtypeStruct((T, LANES), F32),
        ],
        scratch_shapes=[pltpu.VMEM((tm, D), BF16)],
        compiler_params=_cparams("parallel", "arbitrary"),
        name="inproj",
    )(x2d, lnw, w_main, w_ba)


def _conv_body(x_ref, p_ref, w_ref, o_ref, *, n_q_blocks, n_qk_blocks, heads_per_block, q_scale):
    s = pl.program_id(1)
    c = pl.program_id(2)
    x = x_ref[0]
    ts = x.shape[0]
    prev = jnp.where(s > 0, p_ref[0], 0.0)
    w = w_ref[...]
    row8 = lax.broadcasted_iota(jnp.int32, prev.shape, 0)

    main = w[CONV_K - 1:CONV_K, :] * x
    head = w[CONV_K - 1:CONV_K, :] * x[0:8, :]
    for d in range(1, CONV_K):
        wd = w[CONV_K - 1 - d:CONV_K - d, :]
        xr = pltpu.roll(x, d, axis=0)
        main = main + wd * xr
        fix = jnp.where(row8 < d, pltpu.roll(prev, d, axis=0), xr[0:8, :])
        head = head + wd * fix

    def finish(y):
        y = _silu(y)

        def normed(scale):
            parts = []
            for hh in range(heads_per_block):
                yh = y[:, hh * HEAD_DIM:(hh + 1) * HEAD_DIM]
                ss = jnp.sum(yh * yh, axis=-1, keepdims=True)
                parts.append(yh * (lax.rsqrt(ss + EPS) * scale))
            return jnp.concatenate(parts, axis=-1)

        return y, normed

    def store(rows, y):
        ysilu, normed = finish(y)

        @pl.when(c < n_q_blocks)
        def _():
            o_ref[0, rows, :] = normed(q_scale)

        @pl.when(jnp.logical_and(c >= n_q_blocks, c < n_qk_blocks))
        def _():
            o_ref[0, rows, :] = normed(1.0)

        @pl.when(c >= n_qk_blocks)
        def _():
            o_ref[0, rows, :] = ysilu

    store(slice(None), main)
    store(slice(0, 8), head)


def _dn_prep(proj3, conv_w, ts, ct):
    B, S, _ = proj3.shape
    C = conv_w.shape[1]
    body = functools.partial(
        _conv_body,
        n_q_blocks=DN_WIDTH // ct,
        n_qk_blocks=2 * DN_WIDTH // ct,
        heads_per_block=ct // HEAD_DIM,
        q_scale=HEAD_DIM ** -0.5,
    )
    return pl.pallas_call(
        body,
        grid=(B, S // ts, C // ct),
        in_specs=[
            pl.BlockSpec((1, ts, ct), lambda b, s, c: (b, s, c)),
            pl.BlockSpec((1, 8, ct), lambda b, s, c: (b, jnp.maximum(s * (ts // 8) - 1, 0), c)),
            pl.BlockSpec((CONV_K, ct), lambda b, s, c: (0, c)),
        ],
        out_specs=pl.BlockSpec((1, ts, ct), lambda b, s, c: (b, s, c)),
        out_shape=jax.ShapeDtypeStruct((B, S, C), F32),
        compiler_params=_cparams("parallel", "parallel", "parallel"),
        name="dn_prep",
    )(proj3, proj3, conv_w)


def _bdot(a, b):
    return jnp.dot(a.astype(BF16), b.astype(BF16), preferred_element_type=F32)


def _bdot_nt(a, b):
    return lax.dot_general(a.astype(BF16), b.astype(BF16), (((1,), (1,)), ((), ())),
                           preferred_element_type=F32)


def _bdot_tn(a, b):
    return lax.dot_general(a.astype(BF16), b.astype(BF16), (((0,), (0,)), ((), ())),
                           preferred_element_type=F32)


def _gdn_body(alog_ref, dtb_ref, da_ref, db_ref, q_ref, k_ref, v_ref, z_ref, nw_ref,
              o_ref, gc_sc, beta_sc, state_sc, *, n_chunks):
    s = pl.program_id(2)
    C = CHUNK

    @pl.when(s == 0)
    def _init():
        x = da_ref[0, 0] + dtb_ref[0]
        softplus = jnp.maximum(x, 0.0) + jnp.log1p(jnp.exp(-jnp.abs(x)))
        g = -jnp.exp(alog_ref[0]) * softplus
        row = lax.broadcasted_iota(jnp.int32, g.shape, 0)
        k = 1
        while k < C:
            g = g + jnp.where(row >= k, pltpu.roll(g, k, axis=0), 0.0)
            k *= 2
        gc_sc[...] = g
        beta_sc[...] = 1.0 / (1.0 + jnp.exp(-db_ref[0, 0]))
        state_sc[...] = jnp.zeros_like(state_sc)

    ii = lax.broadcasted_iota(jnp.int32, (C, C), 0)
    jj = lax.broadcasted_iota(jnp.int32, (C, C), 1)
    tril = ii >= jj
    strict = ii > jj
    eye = ii == jj
    lane = lax.broadcasted_iota(jnp.int32, gc_sc.shape, 1)
    nw = nw_ref[...]

    def chunk_step(c, state):
        rows = pl.ds(pl.multiple_of(c * C, C), C)
        cg = s * n_chunks + c
        sel = lane == cg
        gcol = jnp.sum(jnp.where(sel, gc_sc[...], 0.0), axis=1, keepdims=True)
        bcol = jnp.sum(jnp.where(sel, beta_sc[...], 0.0), axis=1, keepdims=True)
        grow = jnp.sum(jnp.where(eye, jnp.broadcast_to(gcol, (C, C)), 0.0), axis=0, keepdims=True)
        glast = gcol[C - 1:C, :]
        decay = jnp.where(tril, jnp.exp(jnp.minimum(gcol - grow, 0.0)), 0.0)
        egc = jnp.exp(gcol)

        q = q_ref[0, rows, :]
        k = k_ref[0, rows, :]
        v = v_ref[0, rows, :]
        kb = k * bcol
        a_mat = jnp.where(strict, _bdot_nt(kb, k) * decay, 0.0)
        attn = jnp.where(tril, _bdot_nt(q, k) * decay, 0.0)

        t_inv = jnp.where(eye, 1.0, 0.0) - jnp.where((ii >> 1) == (jj >> 1), a_mat, 0.0)
        lvl = 1
        while (2 << lvl) <= C:
            off = jnp.logical_and((ii >> (lvl + 1)) == (jj >> (lvl + 1)),
                                  ((ii >> lvl) & 1) > ((jj >> lvl) & 1))
            a_off = jnp.where(off, a_mat, 0.0)
            t_inv = t_inv - _bdot(_bdot(t_inv, a_off), t_inv)
            lvl += 1

        rhs = jnp.concatenate([v * bcol, kb * egc], axis=-1)
        sol = _bdot(t_inv, rhs)
        u = sol[:, :HEAD_DIM]
        w = sol[:, HEAD_DIM:]
        v_new = u - _bdot(w, state)
        o = _bdot(q * egc, state) + _bdot(attn, v_new)
        k_dec = k * jnp.exp(glast - gcol)
        state = state * jnp.exp(glast) + _bdot_tn(k_dec, v_new)

        ms = jnp.mean(o * o, axis=-1, keepdims=True)
        z = z_ref[0, rows, :]
        o_ref[0, rows, :] = ((o * lax.rsqrt(ms + EPS)) * nw * _silu(z)).astype(o_ref.dtype)
        return state

    state_sc[...] = lax.fori_loop(0, n_chunks, chunk_step, state_sc[...])


def _gdn(alog_b, dtb_b, da_c, db_c, qkv, proj3, nw, ts):
    B, S, _ = qkv.shape
    H = DN_HEADS
    n_chunks = ts // CHUNK
    n_seq_chunks = S // CHUNK
    z_off = 3 * DN_WIDTH // HEAD_DIM
    body = functools.partial(_gdn_body, n_chunks=n_chunks)
    return pl.pallas_call(
        body,
        grid=(B, H, S // ts),
        in_specs=[
            pl.BlockSpec((1, 1, n_seq_chunks), lambda b, h, s: (h, 0, 0)),
            pl.BlockSpec((1, 1, n_seq_chunks), lambda b, h, s: (h, 0, 0)),
            pl.BlockSpec((1, 1, CHUNK, n_seq_chunks), lambda b, h, s: (b, h, 0, 0)),
            pl.BlockSpec((1, 1, CHUNK, n_seq_chunks), lambda b, h, s: (b, h, 0, 0)),
            pl.BlockSpec((1, ts, HEAD_DIM), lambda b, h, s: (b, s, h)),
            pl.BlockSpec((1, ts, HEAD_DIM), lambda b, h, s: (b, s, H + h)),
            pl.BlockSpec((1, ts, HEAD_DIM), lambda b, h, s: (b, s, 2 * H + h)),
            pl.BlockSpec((1, ts, HEAD_DIM), lambda b, h, s: (b, s, z_off + h)),
            pl.BlockSpec((1, HEAD_DIM), lambda b, h, s: (0, 0)),
        ],
        out_specs=pl.BlockSpec((1, ts, HEAD_DIM), lambda b, h, s: (b, s, h)),
        out_shape=jax.ShapeDtypeStruct((B, S, DN_WIDTH), BF16),
        scratch_shapes=[
            pltpu.VMEM((CHUNK, n_seq_chunks), F32),
            pltpu.VMEM((CHUNK, n_seq_chunks), F32),
            pltpu.VMEM((HEAD_DIM, HEAD_DIM), F32),
        ],
        compiler_params=_cparams("parallel", "parallel", "arbitrary"),
        name="gdn",
    )(alog_b, dtb_b, da_c, db_c, qkv, qkv, qkv, proj3, nw)


def _rope_body(x_ref, cos_ref, sin_ref, o_ref, *, q_scale, n_heads):
    part = pl.program_id(2)

    @pl.when(part < 2)
    def _():
        cos = cos_ref[...]
        sin = sin_ref[...]
        lane = lax.broadcasted_iota(jnp.int32, cos.shape, 1)
        scale = jnp.where(part == 0, q_scale, 1.0).astype(F32)
        half = ROPE_DIM // 2
        for hh in range(n_heads):
            sl = slice(hh * HEAD_DIM, (hh + 1) * HEAD_DIM)
            xh = x_ref[0, :, sl]
            partner = jnp.where(lane < half,
                                pltpu.roll(xh, HEAD_DIM - half, axis=1),
                                pltpu.roll(xh, half, axis=1))
            o_ref[0, :, sl] = ((xh * cos + partner * sin) * scale).astype(o_ref.dtype)

    @pl.when(part == 2)
    def _():
        o_ref[...] = x_ref[...].astype(o_ref.dtype)


def _att_prep(proj3, cos_t, sin_t, ts):
    B, S, _ = proj3.shape
    W = DF_WIDTH
    first = 4 * DN_WIDTH // W
    body = functools.partial(_rope_body, q_scale=HEAD_DIM ** -0.5, n_heads=W // HEAD_DIM)
    return pl.pallas_call(
        body,
        grid=(B, S // ts, 3),
        in_specs=[
            pl.BlockSpec((1, ts, W), lambda b, s, p: (b, s, first + p)),
            pl.BlockSpec((ts, HEAD_DIM), lambda b, s, p: (s, 0)),
            pl.BlockSpec((ts, HEAD_DIM), lambda b, s, p: (s, 0)),
        ],
        out_specs=pl.BlockSpec((1, ts, W), lambda b, s, p: (b, s, p)),
        out_shape=jax.ShapeDtypeStruct((B, S, 3 * W), BF16),
        compiler_params=_cparams("parallel", "parallel", "parallel"),
        name="att_prep",
    )(proj3, cos_t, sin_t)


def _rope_tables(S):
    pos = jnp.arange(S, dtype=F32)
    inv = ROPE_THETA ** (-jnp.arange(0, ROPE_DIM, 2, dtype=F32) / ROPE_DIM)
    ang = pos[:, None] * inv[None, :]
    c, s = jnp.cos(ang), jnp.sin(ang)
    rest = HEAD_DIM - ROPE_DIM
    cos = jnp.concatenate([c, c, jnp.ones((S, rest), F32)], axis=-1)
    sin = jnp.concatenate([-s, s, jnp.zeros((S, rest), F32)], axis=-1)
    return cos, sin


def _datt_body(lq1_ref, lk1_ref, lq2_ref, lk2_ref, nw_ref, q_ref, k_ref, v_ref, o_ref,
               m_sc, l_sc, acc_sc, *, lambda_init, tq, tk):
    qi = pl.program_id(2)
    ki = pl.program_id(3)
    last = (qi * tq + tq - 1) // tk

    @pl.when(ki == 0)
    def _():
        m_sc[...] = jnp.full_like(m_sc, -jnp.inf)
        l_sc[...] = jnp.zeros_like(l_sc)
        acc_sc[...] = jnp.zeros_like(acc_sc)

    def step(masked):
        v = v_ref[0]
        for sub in range(2):
            sl = slice(sub * HEAD_DIM, (sub + 1) * HEAD_DIM)
            s = lax.dot_general(q_ref[0, :, sl], k_ref[0, :, sl], (((1,), (1,)), ((), ())),
                                preferred_element_type=F32)
            if masked:
                qpos = qi * tq + lax.broadcasted_iota(jnp.int32, s.shape, 0)
                kpos = ki * tk + lax.broadcasted_iota(jnp.int32, s.shape, 1)
                s = jnp.where(kpos <= qpos, s, -jnp.inf)
            m_prev = m_sc[sub]
            m_new = jnp.maximum(m_prev, jnp.max(s, axis=-1, keepdims=True))
            alpha = jnp.exp(m_prev - m_new)
            p = jnp.exp(s - m_new)
            l_sc[sub] = alpha * l_sc[sub] + jnp.sum(p, axis=-1, keepdims=True)
            acc_sc[sub] = alpha * acc_sc[sub] + jnp.dot(p.astype(BF16), v, preferred_element_type=F32)
            m_sc[sub] = m_new

    needs_mask = (ki * tk + tk - 1) > (qi * tq)

    @pl.when(jnp.logical_and(ki <= last, needs_mask))
    def _():
        step(True)

    @pl.when(jnp.logical_and(ki <= last, jnp.logical_not(needs_mask)))
    def _():
        step(False)

    @pl.when(ki == last)
    def _():
        lam = (jnp.exp(jnp.sum(lq1_ref[...] * lk1_ref[...], axis=-1, keepdims=True))
               - jnp.exp(jnp.sum(lq2_ref[...] * lk2_ref[...], axis=-1, keepdims=True))
               + lambda_init)
        o = acc_sc[0] / l_sc[0] - lam * (acc_sc[1] / l_sc[1])
        ms = jnp.mean(o * o, axis=-1, keepdims=True)
        o_ref[0] = ((o * lax.rsqrt(ms + EPS)) * nw_ref[...] * (1.0 - lambda_init)).astype(o_ref.dtype)


def _diff_attention(lq1, lk1, lq2, lk2, nw, qkv, lambda_init, tq, tk):
    B, S, _ = qkv.shape
    H = DF_HEADS
    body = functools.partial(_datt_body, lambda_init=lambda_init, tq=tq, tk=tk)

    def kv_idx(off):
        def f(b, h, qi, ki):
            return (b, jnp.minimum(ki, (qi * tq + tq - 1) // tk), off + h)
        return f

    vec = pl.BlockSpec((1, HEAD_DIM), lambda b, h, qi, ki: (0, 0))
    return pl.pallas_call(
        body,
        grid=(B, H, S // tq, S // tk),
        in_specs=[
            vec, vec, vec, vec,
            pl.BlockSpec((1, DF_DV), lambda b, h, qi, ki: (0, 0)),
            pl.BlockSpec((1, tq, DF_DV), lambda b, h, qi, ki: (b, qi, h)),
            pl.BlockSpec((1, tk, DF_DV), kv_idx(H)),
            pl.BlockSpec((1, tk, DF_DV), kv_idx(2 * H)),
        ],
        out_specs=pl.BlockSpec((1, tq, DF_DV), lambda b, h, qi, ki: (b, qi, h)),
        out_shape=jax.ShapeDtypeStruct((B, S, DF_WIDTH), BF16),
        scratch_shapes=[
            pltpu.VMEM((2, tq, 1), F32),
            pltpu.VMEM((2, tq, 1), F32),
            pltpu.VMEM((2, tq, DF_DV), F32),
        ],
        compiler_params=_cparams("parallel", "parallel", "parallel", "arbitrary"),
        name="diff_attn",
    )(lq1, lk1, lq2, lk2, nw, qkv, qkv, qkv)


def _outproj_body(a1_ref, a2_ref, w1_ref, w2_ref, r_ref, o_ref):
    acc = jnp.dot(a1_ref[...], w1_ref[...], preferred_element_type=F32)
    acc = acc + jnp.dot(a2_ref[...], w2_ref[...], preferred_element_type=F32)
    o_ref[...] = r_ref[...] + acc


def _outproj(a1, a2, w, resid, tm, tn):
    T, K1 = a1.shape
    K2 = a2.shape[1]
    N = w.shape[1]
    assert K1 == K2 and w.shape[0] == K1 + K2
    return pl.pallas_call(
        _outproj_body,
        grid=(T // tm, N // tn),
        in_specs=[
            pl.BlockSpec((tm, K1), lambda i, j: (i, 0)),
            pl.BlockSpec((tm, K2), lambda i, j: (i, 0)),
            pl.BlockSpec((K1, tn), lambda i, j: (0, j)),
            pl.BlockSpec((K2, tn), lambda i, j: (1, j)),
            pl.BlockSpec((tm, tn), lambda i, j: (i, j)),
        ],
        out_specs=pl.BlockSpec((tm, tn), lambda i, j: (i, j)),
        out_shape=jax.ShapeDtypeStruct((T, N), F32),
        compiler_params=_cparams("parallel", "parallel"),
        name="outproj",
    )(a1, a2, w, w, resid)


def _ffn_up_body(x_ref, lnw_ref, wg_ref, wu_ref, o_ref, h_ref):
    @pl.when(pl.program_id(1) == 0)
    def _():
        x = x_ref[...]
        ms = jnp.mean(x * x, axis=-1, keepdims=True)
        h_ref[...] = (x * lax.rsqrt(ms + EPS) * lnw_ref[...]).astype(BF16)

    h = h_ref[...]
    g = jnp.dot(h, wg_ref[...], preferred_element_type=F32)
    u = jnp.dot(h, wu_ref[...], preferred_element_type=F32)
    o_ref[...] = (_silu(g) * u).astype(o_ref.dtype)


def _ffn_up(x2d, lnw, wg, wu, tm, tn):
    T, D = x2d.shape
    N = wg.shape[1]
    return pl.pallas_call(
        _ffn_up_body,
        grid=(T // tm, N // tn),
        in_specs=[
            pl.BlockSpec((tm, D), lambda i, j: (i, 0)),
            pl.BlockSpec((1, D), lambda i, j: (0, 0)),
            pl.BlockSpec((D, tn), lambda i, j: (0, j)),
            pl.BlockSpec((D, tn), lambda i, j: (0, j)),
        ],
        out_specs=pl.BlockSpec((tm, tn), lambda i, j: (i, j)),
        out_shape=jax.ShapeDtypeStruct((T, N), BF16),
        scratch_shapes=[pltpu.VMEM((tm, D), BF16)],
        compiler_params=_cparams("parallel", "arbitrary"),
        name="ffn_up",
    )(x2d, lnw, wg, wu)


def _ffn_down_body(a_ref, w_ref, r_ref, o_ref):
    d = jnp.dot(a_ref[...], w_ref[...], preferred_element_type=F32)

    @pl.when(pl.program_id(2) == 0)
    def _():
        o_ref[...] = r_ref[...] + d

    @pl.when(pl.program_id(2) != 0)
    def _():
        o_ref[...] = o_ref[...] + d


def _ffn_down(a, w, resid, tm, tn, tk):
    T, K = a.shape
    N = w.shape[1]
    return pl.pallas_call(
        _ffn_down_body,
        grid=(T // tm, N // tn, K // tk),
        in_specs=[
            pl.BlockSpec((tm, tk), lambda i, j, k: (i, k)),
            pl.BlockSpec((tk, tn), lambda i, j, k: (k, j)),
            pl.BlockSpec((tm, tn), lambda i, j, k: (i, j)),
        ],
        out_specs=pl.BlockSpec((tm, tn), lambda i, j, k: (i, j)),
        out_shape=jax.ShapeDtypeStruct((T, N), F32),
        compiler_params=_cparams("parallel", "parallel", "arbitrary"),
        name="ffn_down",
    )(a, w, resid)


def _rmsnorm_body(x_ref, w_ref, o_ref):
    x = x_ref[...]
    ms = jnp.mean(x * x, axis=-1, keepdims=True)
    o_ref[...] = x * lax.rsqrt(ms + EPS) * w_ref[...]


def _rmsnorm(x2d, w, tm):
    T, D = x2d.shape
    return pl.pallas_call(
        _rmsnorm_body,
        grid=(T // tm,),
        in_specs=[pl.BlockSpec((tm, D), lambda i: (i, 0)), pl.BlockSpec((1, D), lambda i: (0, 0))],
        out_specs=pl.BlockSpec((tm, D), lambda i: (i, 0)),
        out_shape=jax.ShapeDtypeStruct((T, D), F32),
        compiler_params=_cparams("parallel"),
        name="final_norm",
    )(x2d, w)


def _layer(x, l, ln_mix_w, w_in, conv_w, a_log, dt_bias, dn_norm_w, lambda_q1, lambda_k1, lambda_q2,
           lambda_k2, df_norm_w, w_out, ln_ffn_w, w_gate, w_up, w_down, cos_t, sin_t):
    B, S, D = x.shape
    T = B * S
    x2d = x.reshape(T, D)

    n_a = 4 * DN_WIDTH
    n_ba = 2 * DN_HEADS
    w_main = jnp.concatenate([w_in[:, :n_a], w_in[:, n_a + n_ba:]], axis=1).astype(BF16)
    w_ba = jnp.pad(w_in[:, n_a:n_a + n_ba], ((0, 0), (0, LANES - n_ba))).astype(BF16)

    proj, ba = _inproj(x2d, ln_mix_w.reshape(1, D), w_main, w_ba, tm=min(512, T), tn=1024)
    proj3 = proj.reshape(B, S, -1)

    qkv_dn = _dn_prep(proj3, conv_w, ts=min(512, S), ct=1024)
    n_sc = S // CHUNK
    ba4 = ba.reshape(B, n_sc, CHUNK, LANES)
    db_c = ba4[..., :DN_HEADS].transpose(0, 3, 2, 1)
    da_c = ba4[..., DN_HEADS:n_ba].transpose(0, 3, 2, 1)
    alog_b = jnp.broadcast_to(a_log.astype(F32)[:, None, None], (DN_HEADS, 1, n_sc))
    dtb_b = jnp.broadcast_to(dt_bias.astype(F32)[:, None, None], (DN_HEADS, 1, n_sc))
    o_dn = _gdn(alog_b, dtb_b, da_c, db_c, qkv_dn, proj3, dn_norm_w.reshape(1, HEAD_DIM), ts=min(512, S))

    lambda_init = 0.8 - 0.6 * math.exp(-0.3 * l)
    qkv_att = _att_prep(proj3, cos_t, sin_t, ts=min(512, S))
    o_df = _diff_attention(lambda_q1.reshape(1, -1), lambda_k1.reshape(1, -1), lambda_q2.reshape(1, -1),
                           lambda_k2.reshape(1, -1), df_norm_w.reshape(1, -1), qkv_att, lambda_init,
                           tq=min(512, S), tk=min(512, S))

    x1 = _outproj(o_dn.reshape(T, -1), o_df.reshape(T, -1), w_out.astype(BF16), x2d, tm=min(1024, T), tn=512)

    act = _ffn_up(x1, ln_ffn_w.reshape(1, D), w_gate.astype(BF16), w_up.astype(BF16), tm=min(512, T), tn=256)
    x2 = _ffn_down(act, w_down.astype(BF16), x1, tm=min(1024, T), tn=512, tk=w_down.shape[0] // 2)
    return x2.reshape(B, S, D)


def kernel(x, ln_mix_w, w_in, conv_w, a_log, dt_bias, dn_norm_w, lambda_q1, lambda_k1, lambda_q2, lambda_k2,
           df_norm_w, w_out, ln_ffn_w, w_gate, w_up, w_down, ln_final_w):
    B, S, D = x.shape
    cos_t, sin_t = _rope_tables(S)
    depth = w_in.shape[0]
    for l in range(depth):
        x = _layer(x, l, ln_mix_w[l], w_in[l], conv_w[l], a_log[l], dt_bias[l], dn_norm_w[l], lambda_q1[l],
                   lambda_k1[l], lambda_q2[l], lambda_k2[l], df_norm_w[l], w_out[l], ln_ffn_w[l], w_gate[l],
                   w_up[l], w_down[l], cos_t, sin_t)
    out = _rmsnorm(x.reshape(B * S, D), ln_final_w.reshape(1, D), tm=min(256, B * S))
    return out.reshape(B, S, D)
```

```python
import functools
import math

import jax
import jax.numpy as jnp
import numpy as np
from jax import lax
from jax.experimental import pallas as pl
from jax.experimental.pallas import tpu as pltpu

F32 = jnp.float32
BF16 = jnp.bfloat16

EPS = 1e-6
HEAD_DIM = 128
DN_HEADS = 16
DN_WIDTH = DN_HEADS * HEAD_DIM
CONV_K = 4
GDN_CHUNK = 128
GDN_TILE = 1024
GDN_HEADS_PER_STEP = 4
DF_HEADS = 8
DF_DV = 2 * HEAD_DIM
DF_WIDTH = DF_HEADS * DF_DV
ROPE_THETA = 500000.0
ROPE_DIM = HEAD_DIM // 4
LANES = 128
VMEM_LIMIT = 56 * 1024 * 1024


def _cparams(*sem):
    return pltpu.CompilerParams(dimension_semantics=sem, vmem_limit_bytes=VMEM_LIMIT)


def _silu(x):
    return x / (1.0 + jnp.exp(-x))


def _inproj_body(x_ref, lnw_ref, w_ref, wba_ref, o_ref, oba_ref, h_ref):
    @pl.when(pl.program_id(1) == 0)
    def _():
        x = x_ref[...]
        ms = jnp.mean(x * x, axis=-1, keepdims=True)
        h = (x * lax.rsqrt(ms + EPS) * lnw_ref[...]).astype(BF16)
        h_ref[...] = h
        oba_ref[...] = jnp.dot(h, wba_ref[...], preferred_element_type=F32)

    o_ref[...] = jnp.dot(h_ref[...], w_ref[...], preferred_element_type=F32).astype(o_ref.dtype)


def _inproj(x2d, lnw, w_main, w_ba, tm, tn):
    T, D = x2d.shape
    N = w_main.shape[1]
    return pl.pallas_call(
        _inproj_body,
        grid=(T // tm, N // tn),
        in_specs=[
            pl.BlockSpec((tm, D), lambda i, j: (i, 0)),
            pl.BlockSpec((1, D), lambda i, j: (0, 0)),
            pl.BlockSpec((D, tn), lambda i, j: (0, j)),
            pl.BlockSpec((D, LANES), lambda i, j: (0, 0)),
        ],
        out_specs=[
            pl.BlockSpec((tm, tn), lambda i, j: (i, j)),
            pl.BlockSpec((tm, LANES), lambda i, j: (i, 0)),
        ],
        out_shape=[
            jax.ShapeDtypeStruct((T, N), F32),
            jax.ShapeDtypeStruct((T, LANES), F32),
        ],
        scratch_shapes=[pltpu.VMEM((tm, D), BF16)],
        compiler_params=_cparams("parallel", "arbitrary"),
        name="inproj",
    )(x2d, lnw, w_main, w_ba)


def _conv_body(x_ref, p_ref, w_ref, o_ref, *, n_q_blocks, n_qk_blocks, heads_per_block, q_scale):
    s = pl.program_id(1)
    c = pl.program_id(2)
    x = x_ref[0]
    ts = x.shape[0]
    prev = jnp.where(s > 0, p_ref[0], 0.0)
    w = w_ref[...]
    row8 = lax.broadcasted_iota(jnp.int32, prev.shape, 0)

    main = w[CONV_K - 1:CONV_K, :] * x
    head = w[CONV_K - 1:CONV_K, :] * x[0:8, :]
    for d in range(1, CONV_K):
        wd = w[CONV_K - 1 - d:CONV_K - d, :]
        xr = pltpu.roll(x, d, axis=0)
        main = main + wd * xr
        fix = jnp.where(row8 < d, pltpu.roll(prev, d, axis=0), xr[0:8, :])
        head = head + wd * fix

    def finish(y):
        y = _silu(y)

        def normed(scale):
            parts = []
            for hh in range(heads_per_block):
                yh = y[:, hh * HEAD_DIM:(hh + 1) * HEAD_DIM]
                ss = jnp.sum(yh * yh, axis=-1, keepdims=True)
                parts.append(yh * (lax.rsqrt(ss + EPS) * scale))
            return jnp.concatenate(parts, axis=-1)

        return y, normed

    def store(rows, y):
        ysilu, normed = finish(y)

        @pl.when(c < n_q_blocks)
        def _():
            o_ref[0, rows, :] = normed(q_scale)

        @pl.when(jnp.logical_and(c >= n_q_blocks, c < n_qk_blocks))
        def _():
            o_ref[0, rows, :] = normed(1.0)

        @pl.when(c >= n_qk_blocks)
        def _():
            o_ref[0, rows, :] = ysilu

    store(slice(None), main)
    store(slice(0, 8), head)


def _dn_prep(proj3, conv_w, ts, ct):
    B, S, _ = proj3.shape
    C = conv_w.shape[1]
    body = functools.partial(
        _conv_body,
        n_q_blocks=DN_WIDTH // ct,
        n_qk_blocks=2 * DN_WIDTH // ct,
        heads_per_block=ct // HEAD_DIM,
        q_scale=HEAD_DIM ** -0.5,
    )
    return pl.pallas_call(
        body,
        grid=(B, S // ts, C // ct),
        in_specs=[
            pl.BlockSpec((1, ts, ct), lambda b, s, c: (b, s, c)),
            pl.BlockSpec((1, 8, ct), lambda b, s, c: (b, jnp.maximum(s * (ts // 8) - 1, 0), c)),
            pl.BlockSpec((CONV_K, ct), lambda b, s, c: (0, c)),
        ],
        out_specs=pl.BlockSpec((1, ts, ct), lambda b, s, c: (b, s, c)),
        out_shape=jax.ShapeDtypeStruct((B, S, C), F32),
        compiler_params=_cparams("parallel", "parallel", "parallel"),
        name="dn_prep",
    )(proj3, proj3, conv_w)


def _bdot(a, b):
    return jnp.dot(a.astype(BF16), b.astype(BF16), preferred_element_type=F32)


def _bdot_nt(a, b):
    return lax.dot_general(a.astype(BF16), b.astype(BF16), (((1,), (1,)), ((), ())),
                           preferred_element_type=F32)


def _bdot_tn(a, b):
    return lax.dot_general(a.astype(BF16), b.astype(BF16), (((0,), (0,)), ((), ())),
                           preferred_element_type=F32)


def _gdn_body(alog_ref, dtb_ref, da_ref, db_ref, q_ref, k_ref, v_ref, z_ref, nw_ref,
              o_ref, gc_sc, beta_sc, state_sc, lhs_sc, n_sc, r_sc, a_sc, *, n_chunks, n_heads, chunk):
    s = pl.program_id(2)
    C = chunk

    @pl.when(s == 0)
    def _init():
        for g in range(n_heads):
            x = da_ref[0, g] + dtb_ref[g]
            softplus = jnp.maximum(x, 0.0) + jnp.log1p(jnp.exp(-jnp.abs(x)))
            gl = -jnp.exp(alog_ref[g]) * softplus
            row = lax.broadcasted_iota(jnp.int32, gl.shape, 0)
            k = 1
            while k < C:
                gl = gl + jnp.where(row >= k, pltpu.roll(gl, k, axis=0), 0.0)
                k *= 2
            gc_sc[g] = gl
            beta_sc[g] = 1.0 / (1.0 + jnp.exp(-db_ref[0, g]))
        state_sc[...] = jnp.zeros_like(state_sc)

    ii = lax.broadcasted_iota(jnp.int32, (C, C), 0)
    jj = lax.broadcasted_iota(jnp.int32, (C, C), 1)
    tril = ii >= jj
    strict = ii > jj
    eye = ii == jj
    lane = lax.broadcasted_iota(jnp.int32, gc_sc.shape[1:], 1)
    nw = nw_ref[...]

    chunks = range(n_chunks)
    D = HEAD_DIM

    def phase1(g):
        cols = slice(g * D, (g + 1) * D)
        gcol, bcol, decay, egc, kdec, a_last = [], [], [], [], [], []
        q, k, v, kb = [], [], [], []
        for c in chunks:
            rows = slice(c * C, (c + 1) * C)
            sel = lane == s * n_chunks + c
            gc_ = jnp.sum(jnp.where(sel, gc_sc[g], 0.0), axis=1, keepdims=True)
            bc_ = jnp.sum(jnp.where(sel, beta_sc[g], 0.0), axis=1, keepdims=True)
            grow = jnp.sum(jnp.where(eye, jnp.broadcast_to(gc_, (C, C)), 0.0), axis=0, keepdims=True)
            glast = gc_[C - 1:C, :]
            gcol.append(gc_)
            bcol.append(bc_)
            decay.append(jnp.where(tril, jnp.exp(jnp.minimum(gc_ - grow, 0.0)), 0.0))
            egc.append(jnp.exp(gc_))
            a_last.append(jnp.exp(glast))
            q.append(q_ref[0, rows, cols])
            k.append(k_ref[0, rows, cols])
            v.append(v_ref[0, rows, cols])
            kb.append(k[c] * bc_)
            kdec.append(k[c] * jnp.exp(glast - gc_))

        kq = [_bdot_nt(jnp.concatenate([kb[c], q[c]], axis=0), k[c]) for c in chunks]
        a_mat = [jnp.where(strict, kq[c][:C] * decay[c], 0.0) for c in chunks]
        attn = [jnp.where(tril, kq[c][C:] * decay[c], 0.0) for c in chunks]

        t_inv = [jnp.where(eye, 1.0, 0.0) - jnp.where((ii >> 1) == (jj >> 1), a_mat[c], 0.0) for c in chunks]
        lvl = 1
        while (2 << lvl) <= C:
            off = jnp.logical_and((ii >> (lvl + 1)) == (jj >> (lvl + 1)),
                                  ((ii >> lvl) & 1) > ((jj >> lvl) & 1))
            ta = [_bdot(t_inv[c], jnp.where(off, a_mat[c], 0.0)) for c in chunks]
            t_inv = [t_inv[c] - _bdot(ta[c], t_inv[c]) for c in chunks]
            lvl += 1

        wu = [_bdot(t_inv[c], jnp.concatenate([kb[c] * egc[c], v[c] * bcol[c]], axis=-1)) for c in chunks]
        aw = [_bdot(attn[c], wu[c]) for c in chunks]
        mn = [_bdot_tn(kdec[c], wu[c]) for c in chunks]
        for c in chunks:
            p = q[c] * egc[c] - aw[c][:, :D]
            lhs_sc[g, c] = jnp.concatenate([mn[c][:, :D], p], axis=0).astype(BF16)
            n_sc[g, c] = mn[c][:, D:]
            r_sc[g, c] = aw[c][:, D:]
            a_sc[g, c] = jnp.broadcast_to(a_last[c], (8, D))

    for g in range(n_heads):
        phase1(g)

    state = [state_sc[g] for g in range(n_heads)]
    for c in chunks:
        rows = slice(c * C, (c + 1) * C)
        ms_o = [_bdot(lhs_sc[g, c], state[g]) for g in range(n_heads)]
        for g in range(n_heads):
            cols = slice(g * D, (g + 1) * D)
            state[g] = state[g] * a_sc[g, c][0:1, :] - ms_o[g][:D] + n_sc[g, c]
            o = ms_o[g][D:] + r_sc[g, c]
            ms = jnp.mean(o * o, axis=-1, keepdims=True)
            z = z_ref[0, rows, cols]
            o_ref[0, rows, cols] = ((o * lax.rsqrt(ms + EPS)) * nw * _silu(z)).astype(o_ref.dtype)
    for g in range(n_heads):
        state_sc[g] = state[g]


def _gdn(alog_b, dtb_b, da_c, db_c, qkv, proj3, nw, ts, n_heads, chunk):
    B, S, _ = qkv.shape
    G = n_heads
    HG = DN_HEADS // G
    n_chunks = ts // chunk
    n_seq_chunks = S // chunk
    wblk = G * HEAD_DIM
    z_off = 3 * DN_WIDTH // wblk
    body = functools.partial(_gdn_body, n_chunks=n_chunks, n_heads=G, chunk=chunk)
    return pl.pallas_call(
        body,
        grid=(B, HG, S // ts),
        in_specs=[
            pl.BlockSpec((G, 1, n_seq_chunks), lambda b, h, s: (h, 0, 0)),
            pl.BlockSpec((G, 1, n_seq_chunks), lambda b, h, s: (h, 0, 0)),
            pl.BlockSpec((1, G, chunk, n_seq_chunks), lambda b, h, s: (b, h, 0, 0)),
            pl.BlockSpec((1, G, chunk, n_seq_chunks), lambda b, h, s: (b, h, 0, 0)),
            pl.BlockSpec((1, ts, wblk), lambda b, h, s: (b, s, h)),
            pl.BlockSpec((1, ts, wblk), lambda b, h, s: (b, s, HG + h)),
            pl.BlockSpec((1, ts, wblk), lambda b, h, s: (b, s, 2 * HG + h)),
            pl.BlockSpec((1, ts, wblk), lambda b, h, s: (b, s, z_off + h)),
            pl.BlockSpec((1, HEAD_DIM), lambda b, h, s: (0, 0)),
        ],
        out_specs=pl.BlockSpec((1, ts, wblk), lambda b, h, s: (b, s, h)),
        out_shape=jax.ShapeDtypeStruct((B, S, DN_WIDTH), BF16),
        scratch_shapes=[
            pltpu.VMEM((G, chunk, n_seq_chunks), F32),
            pltpu.VMEM((G, chunk, n_seq_chunks), F32),
            pltpu.VMEM((G, HEAD_DIM, HEAD_DIM), F32),
            pltpu.VMEM((G, n_chunks, HEAD_DIM + chunk, HEAD_DIM), BF16),
            pltpu.VMEM((G, n_chunks, HEAD_DIM, HEAD_DIM), F32),
            pltpu.VMEM((G, n_chunks, chunk, HEAD_DIM), F32),
            pltpu.VMEM((G, n_chunks, 8, HEAD_DIM), F32),
        ],
        compiler_params=_cparams("parallel", "parallel", "arbitrary"),
        name="gdn",
    )(alog_b, dtb_b, da_c, db_c, qkv, qkv, qkv, proj3, nw)


def _rope_body(x_ref, cos_ref, sin_ref, o_ref, *, q_scale, n_heads):
    part = pl.program_id(2)

    @pl.when(part < 2)
    def _():
        cos = cos_ref[...]
        sin = sin_ref[...]
        lane = lax.broadcasted_iota(jnp.int32, cos.shape, 1)
        scale = jnp.where(part == 0, q_scale, 1.0).astype(F32)
        half = ROPE_DIM // 2
        for hh in range(n_heads):
            sl = slice(hh * HEAD_DIM, (hh + 1) * HEAD_DIM)
            xh = x_ref[0, :, sl]
            partner = jnp.where(lane < half,
                                pltpu.roll(xh, HEAD_DIM - half, axis=1),
                                pltpu.roll(xh, half, axis=1))
            o_ref[0, :, sl] = ((xh * cos + partner * sin) * scale).astype(o_ref.dtype)

    @pl.when(part == 2)
    def _():
        o_ref[...] = x_ref[...].astype(o_ref.dtype)


def _att_prep(proj3, cos_t, sin_t, ts):
    B, S, _ = proj3.shape
    W = DF_WIDTH
    first = 4 * DN_WIDTH // W
    body = functools.partial(_rope_body, q_scale=HEAD_DIM ** -0.5 * math.log2(math.e), n_heads=W // HEAD_DIM)
    return pl.pallas_call(
        body,
        grid=(B, S // ts, 3),
        in_specs=[
            pl.BlockSpec((1, ts, W), lambda b, s, p: (b, s, first + p)),
            pl.BlockSpec((ts, HEAD_DIM), lambda b, s, p: (s, 0)),
            pl.BlockSpec((ts, HEAD_DIM), lambda b, s, p: (s, 0)),
        ],
        out_specs=pl.BlockSpec((1, ts, W), lambda b, s, p: (b, s, p)),
        out_shape=jax.ShapeDtypeStruct((B, S, 3 * W), BF16),
        compiler_params=_cparams("parallel", "parallel", "parallel"),
        name="att_prep",
    )(proj3, cos_t, sin_t)


def _rope_tables(S):
    pos = jnp.arange(S, dtype=F32)
    inv = ROPE_THETA ** (-jnp.arange(0, ROPE_DIM, 2, dtype=F32) / ROPE_DIM)
    ang = pos[:, None] * inv[None, :]
    c, s = jnp.cos(ang), jnp.sin(ang)
    rest = HEAD_DIM - ROPE_DIM
    cos = jnp.concatenate([c, c, jnp.ones((S, rest), F32)], axis=-1)
    sin = jnp.concatenate([-s, s, jnp.zeros((S, rest), F32)], axis=-1)
    return cos, sin


def _datt_body(lq1_ref, lk1_ref, lq2_ref, lk2_ref, nw_ref, q_ref, k_ref, v_ref, o_ref,
               m_sc, l_sc, acc_sc, s_sc, *, lambda_init, tq, tk):
    qi = pl.program_id(2)
    m_sc[...] = jnp.full_like(m_sc, -jnp.inf)
    l_sc[...] = jnp.zeros_like(l_sc)
    acc_sc[...] = jnp.zeros_like(acc_sc)

    def kv_rows(ki):
        return pl.ds(pl.multiple_of(ki * tk, tk), tk)

    def scores(ki, slot):
        for sub in range(2):
            sl = slice(sub * HEAD_DIM, (sub + 1) * HEAD_DIM)
            s_sc[slot, sub] = lax.dot_general(q_ref[0, :, sl], k_ref[0, kv_rows(ki), sl],
                                              (((1,), (1,)), ((), ())), preferred_element_type=F32)

    def softmax_pv(ki, slot, masked):
        v = v_ref[0, kv_rows(ki), :]
        for sub in range(2):
            s = s_sc[slot, sub]
            if masked:
                row = lax.broadcasted_iota(jnp.int32, s.shape, 0)
                col = lax.broadcasted_iota(jnp.int32, s.shape, 1)
                s = jnp.where(col <= row, s, -jnp.inf)
            m_prev = m_sc[sub]
            m_new = jnp.maximum(m_prev, jnp.max(s, axis=-1, keepdims=True))
            alpha = jnp.exp2(m_prev - m_new)
            p = jnp.exp2(s - pltpu.repeat(m_new, tk // LANES, axis=1))
            l_sc[sub] = alpha * l_sc[sub] + jnp.sum(p, axis=-1, keepdims=True)
            acc_sc[sub] = (pltpu.repeat(alpha, DF_DV // LANES, axis=1) * acc_sc[sub]
                           + jnp.dot(p.astype(BF16), v, preferred_element_type=F32))
            m_sc[sub] = m_new

    scores(0, 0)

    def pair(j, carry):
        k0 = 2 * j
        scores(k0 + 1, 1)
        softmax_pv(k0, 0, False)
        scores(k0 + 2, 0)
        softmax_pv(k0 + 1, 1, False)
        return carry

    lax.fori_loop(0, qi // 2, pair, 0)

    @pl.when(qi % 2 == 1)
    def _():
        scores(qi, 1)
        softmax_pv(qi - 1, 0, False)
        softmax_pv(qi, 1, True)

    @pl.when(qi % 2 == 0)
    def _():
        softmax_pv(qi, 0, True)

    lam = (jnp.exp(jnp.sum(lq1_ref[...] * lk1_ref[...], axis=-1, keepdims=True))
           - jnp.exp(jnp.sum(lq2_ref[...] * lk2_ref[...], axis=-1, keepdims=True))
           + lambda_init)
    reps = DF_DV // LANES
    o = (acc_sc[0] / pltpu.repeat(l_sc[0], reps, axis=1)
         - lam * (acc_sc[1] / pltpu.repeat(l_sc[1], reps, axis=1)))
    ms = jnp.mean(o * o, axis=-1, keepdims=True)
    o_ref[0] = ((o * lax.rsqrt(ms + EPS)) * nw_ref[...] * (1.0 - lambda_init)).astype(o_ref.dtype)


def _diff_attention(lq1, lk1, lq2, lk2, nw, qkv, lambda_init, tq):
    B, S, _ = qkv.shape
    H = DF_HEADS
    body = functools.partial(_datt_body, lambda_init=lambda_init, tq=tq, tk=tq)
    vec = pl.BlockSpec((1, HEAD_DIM), lambda b, h, qi: (0, 0))
    return pl.pallas_call(
        body,
        grid=(B, H, S // tq),
        in_specs=[
            vec, vec, vec, vec,
            pl.BlockSpec((1, DF_DV), lambda b, h, qi: (0, 0)),
            pl.BlockSpec((1, tq, DF_DV), lambda b, h, qi: (b, qi, h)),
            pl.BlockSpec((1, S, DF_DV), lambda b, h, qi: (b, 0, H + h)),
            pl.BlockSpec((1, S, DF_DV), lambda b, h, qi: (b, 0, 2 * H + h)),
        ],
        out_specs=pl.BlockSpec((1, tq, DF_DV), lambda b, h, qi: (b, qi, h)),
        out_shape=jax.ShapeDtypeStruct((B, S, DF_WIDTH), BF16),
        scratch_shapes=[
            pltpu.VMEM((2, tq, LANES), F32),
            pltpu.VMEM((2, tq, LANES), F32),
            pltpu.VMEM((2, tq, DF_DV), F32),
            pltpu.VMEM((2, 2, tq, tq), F32),
        ],
        compiler_params=_cparams("parallel", "parallel", "arbitrary"),
        name="diff_attn",
    )(lq1, lk1, lq2, lk2, nw, qkv, qkv, qkv)


def _outproj_body(a1_ref, a2_ref, w1_ref, w2_ref, r_ref, o_ref):
    acc = jnp.dot(a1_ref[...], w1_ref[...], preferred_element_type=F32)
    acc = acc + jnp.dot(a2_ref[...], w2_ref[...], preferred_element_type=F32)
    o_ref[...] = r_ref[...] + acc


def _outproj(a1, a2, w, resid, tm, tn):
    T, K1 = a1.shape
    K2 = a2.shape[1]
    N = w.shape[1]
    assert K1 == K2 and w.shape[0] == K1 + K2
    return pl.pallas_call(
        _outproj_body,
        grid=(T // tm, N // tn),
        in_specs=[
            pl.BlockSpec((tm, K1), lambda i, j: (i, 0)),
            pl.BlockSpec((tm, K2), lambda i, j: (i, 0)),
            pl.BlockSpec((K1, tn), lambda i, j: (0, j)),
            pl.BlockSpec((K2, tn), lambda i, j: (1, j)),
            pl.BlockSpec((tm, tn), lambda i, j: (i, j)),
        ],
        out_specs=pl.BlockSpec((tm, tn), lambda i, j: (i, j)),
        out_shape=jax.ShapeDtypeStruct((T, N), F32),
        compiler_params=_cparams("parallel", "parallel"),
        name="outproj",
    )(a1, a2, w, w, resid)


def _ffn_up_body(x_ref, lnw_ref, wg_ref, wu_ref, o_ref, h_ref):
    @pl.when(pl.program_id(1) == 0)
    def _():
        x = x_ref[...]
        ms = jnp.mean(x * x, axis=-1, keepdims=True)
        h_ref[...] = (x * lax.rsqrt(ms + EPS) * lnw_ref[...]).astype(BF16)

    h = h_ref[...]
    g = jnp.dot(h, wg_ref[...], preferred_element_type=F32)
    u = jnp.dot(h, wu_ref[...], preferred_element_type=F32)
    o_ref[...] = (_silu(g) * u).astype(o_ref.dtype)


def _ffn_up(x2d, lnw, wg, wu, tm, tn):
    T, D = x2d.shape
    N = wg.shape[1]
    return pl.pallas_call(
        _ffn_up_body,
        grid=(T // tm, N // tn),
        in_specs=[
            pl.BlockSpec((tm, D), lambda i, j: (i, 0)),
            pl.BlockSpec((1, D), lambda i, j: (0, 0)),
            pl.BlockSpec((D, tn), lambda i, j: (0, j)),
            pl.BlockSpec((D, tn), lambda i, j: (0, j)),
        ],
        out_specs=pl.BlockSpec((tm, tn), lambda i, j: (i, j)),
        out_shape=jax.ShapeDtypeStruct((T, N), BF16),
        scratch_shapes=[pltpu.VMEM((tm, D), BF16)],
        compiler_params=_cparams("parallel", "arbitrary"),
        name="ffn_up",
    )(x2d, lnw, wg, wu)


def _ffn_down_body(a_ref, w_ref, r_ref, o_ref):
    d = jnp.dot(a_ref[...], w_ref[...], preferred_element_type=F32)

    @pl.when(pl.program_id(2) == 0)
    def _():
        o_ref[...] = r_ref[...] + d

    @pl.when(pl.program_id(2) != 0)
    def _():
        o_ref[...] = o_ref[...] + d


def _ffn_down(a, w, resid, tm, tn, tk):
    T, K = a.shape
    N = w.shape[1]
    return pl.pallas_call(
        _ffn_down_body,
        grid=(T // tm, N // tn, K // tk),
        in_specs=[
            pl.BlockSpec((tm, tk), lambda i, j, k: (i, k)),
            pl.BlockSpec((tk, tn), lambda i, j, k: (k, j)),
            pl.BlockSpec((tm, tn), lambda i, j, k: (i, j)),
        ],
        out_specs=pl.BlockSpec((tm, tn), lambda i, j, k: (i, j)),
        out_shape=jax.ShapeDtypeStruct((T, N), F32),
        compiler_params=_cparams("parallel", "parallel", "arbitrary"),
        name="ffn_down",
    )(a, w, resid)


def _rmsnorm_body(x_ref, w_ref, o_ref):
    x = x_ref[...]
    ms = jnp.mean(x * x, axis=-1, keepdims=True)
    o_ref[...] = x * lax.rsqrt(ms + EPS) * w_ref[...]


def _rmsnorm(x2d, w, tm):
    T, D = x2d.shape
    return pl.pallas_call(
        _rmsnorm_body,
        grid=(T // tm,),
        in_specs=[pl.BlockSpec((tm, D), lambda i: (i, 0)), pl.BlockSpec((1, D), lambda i: (0, 0))],
        out_specs=pl.BlockSpec((tm, D), lambda i: (i, 0)),
        out_shape=jax.ShapeDtypeStruct((T, D), F32),
        compiler_params=_cparams("parallel"),
        name="final_norm",
    )(x2d, w)


def _layer(x, l, ln_mix_w, w_in, conv_w, a_log, dt_bias, dn_norm_w, lambda_q1, lambda_k1, lambda_q2,
           lambda_k2, df_norm_w, w_out, ln_ffn_w, w_gate, w_up, w_down, cos_t, sin_t):
    B, S, D = x.shape
    T = B * S
    x2d = x.reshape(T, D)

    n_a = 4 * DN_WIDTH
    n_ba = 2 * DN_HEADS
    w_main = jnp.concatenate([w_in[:, :n_a], w_in[:, n_a + n_ba:]], axis=1).astype(BF16)
    w_ba = jnp.pad(w_in[:, n_a:n_a + n_ba], ((0, 0), (0, LANES - n_ba))).astype(BF16)

    proj, ba = _inproj(x2d, ln_mix_w.reshape(1, D), w_main, w_ba, tm=min(512, T), tn=1024)
    proj3 = proj.reshape(B, S, -1)

    qkv_dn = _dn_prep(proj3, conv_w, ts=min(512, S), ct=1024)
    n_sc = S // GDN_CHUNK
    ba4 = ba.reshape(B, n_sc, GDN_CHUNK, LANES)
    db_c = ba4[..., :DN_HEADS].transpose(0, 3, 2, 1)
    da_c = ba4[..., DN_HEADS:n_ba].transpose(0, 3, 2, 1)
    alog_b = jnp.broadcast_to(a_log.astype(F32)[:, None, None], (DN_HEADS, 1, n_sc))
    dtb_b = jnp.broadcast_to(dt_bias.astype(F32)[:, None, None], (DN_HEADS, 1, n_sc))
    o_dn = _gdn(alog_b, dtb_b, da_c, db_c, qkv_dn, proj3, dn_norm_w.reshape(1, HEAD_DIM),
                ts=min(GDN_TILE, S), n_heads=GDN_HEADS_PER_STEP, chunk=GDN_CHUNK)

    lambda_init = 0.8 - 0.6 * math.exp(-0.3 * l)
    qkv_att = _att_prep(proj3, cos_t, sin_t, ts=min(512, S))
    o_df = _diff_attention(lambda_q1.reshape(1, -1), lambda_k1.reshape(1, -1), lambda_q2.reshape(1, -1),
                           lambda_k2.reshape(1, -1), df_norm_w.reshape(1, -1), qkv_att, lambda_init,
                           tq=min(512, S))

    x1 = _outproj(o_dn.reshape(T, -1), o_df.reshape(T, -1), w_out.astype(BF16), x2d, tm=min(1024, T), tn=512)

    act = _ffn_up(x1, ln_ffn_w.reshape(1, D), w_gate.astype(BF16), w_up.astype(BF16), tm=min(512, T), tn=256)
    x2 = _ffn_down(act, w_down.astype(BF16), x1, tm=min(1024, T), tn=512, tk=w_down.shape[0] // 2)
    return x2.reshape(B, S, D)


def kernel(x, ln_mix_w, w_in, conv_w, a_log, dt_bias, dn_norm_w, lambda_q1, lambda_k1, lambda_q2, lambda_k2,
           df_norm_w, w_out, ln_ffn_w, w_gate, w_up, w_down, ln_final_w):
    B, S, D = x.shape
    cos_t, sin_t = _rope_tables(S)
    depth = w_in.shape[0]
    for l in range(depth):
        x = _layer(x, l, ln_mix_w[l], w_in[l], conv_w[l], a_log[l], dt_bias[l], dn_norm_w[l], lambda_q1[l],
                   lambda_k1[l], lambda_q2[l], lambda_k2[l], df_norm_w[l], w_out[l], ln_ffn_w[l], w_gate[l],
                   w_up[l], w_down[l], cos_t, sin_t)
    out = _rmsnorm(x.reshape(B * S, D), ln_final_w.reshape(1, D), tm=min(256, B * S))
    return out.reshape(B, S, D)
```

```python
import functools
import math

import jax
import jax.numpy as jnp
import numpy as np
from jax import lax
from jax.experimental import pallas as pl
from jax.experimental.pallas import tpu as pltpu

F32 = jnp.float32
BF16 = jnp.bfloat16

EPS = 1e-6
HEAD_DIM = 128
DN_HEADS = 16
DN_WIDTH = DN_HEADS * HEAD_DIM
CONV_K = 4
GDN_CHUNK = 128
GDN_TILE = 1024
GDN_HEADS_PER_STEP = 4
FFN_UP_ROWS = 2048
DF_HEADS = 8
DF_DV = 2 * HEAD_DIM
DF_WIDTH = DF_HEADS * DF_DV
ROPE_THETA = 500000.0
ROPE_DIM = HEAD_DIM // 4
LANES = 128
VMEM_LIMIT = 56 * 1024 * 1024


def _cparams(*sem):
    return pltpu.CompilerParams(dimension_semantics=sem, vmem_limit_bytes=VMEM_LIMIT)


def _silu(x):
    return x / (1.0 + jnp.exp(-x))


def _lane_tile(x, n):
    return x if n == 1 else jnp.concatenate([x] * n, axis=1)


def _normed_rows(x_ref, lnw_ref):
    x = x_ref[...]
    ms = jnp.mean(x * x, axis=-1, keepdims=True)
    return (x * lax.rsqrt(ms + EPS) * lnw_ref[...]).astype(BF16)


def _inproj_dn_body(x_ref, lnw_ref, w_ref, wba_ref, o_ref, oba_ref, h_ref):
    @pl.when(pl.program_id(1) == 0)
    def _():
        h = _normed_rows(x_ref, lnw_ref)
        h_ref[...] = h
        oba_ref[...] = jnp.dot(h, wba_ref[...], preferred_element_type=F32)

    o_ref[...] = jnp.dot(h_ref[...], w_ref[...], preferred_element_type=F32).astype(o_ref.dtype)


def _inproj_att_body(x_ref, lnw_ref, w_ref, cos_ref, sin_ref, o_ref, h_ref, *, n_q_blocks, n_rope_blocks, q_scale):
    j = pl.program_id(1)

    @pl.when(j == 0)
    def _():
        h_ref[...] = _normed_rows(x_ref, lnw_ref)

    acc = jnp.dot(h_ref[...], w_ref[...], preferred_element_type=F32)

    @pl.when(j < n_rope_blocks)
    def _():
        cos = cos_ref[...]
        sin = sin_ref[...]
        lane = lax.broadcasted_iota(jnp.int32, cos.shape, 1)
        scale = jnp.where(j < n_q_blocks, q_scale, 1.0).astype(F32)
        half = ROPE_DIM // 2
        for hh in range(acc.shape[1] // HEAD_DIM):
            sl = slice(hh * HEAD_DIM, (hh + 1) * HEAD_DIM)
            xh = acc[:, sl]
            partner = jnp.where(lane < half,
                                pltpu.roll(xh, HEAD_DIM - half, axis=1),
                                pltpu.roll(xh, half, axis=1))
            o_ref[:, sl] = ((xh * cos + partner * sin) * scale).astype(o_ref.dtype)

    @pl.when(j >= n_rope_blocks)
    def _():
        o_ref[...] = acc.astype(o_ref.dtype)


def _inproj_dn(x2d, lnw, w_dn, w_ba, tm, tn):
    T, D = x2d.shape
    N = w_dn.shape[1]
    return pl.pallas_call(
        _inproj_dn_body,
        grid=(T // tm, N // tn),
        in_specs=[
            pl.BlockSpec((tm, D), lambda i, j: (i, 0)),
            pl.BlockSpec((1, D), lambda i, j: (0, 0)),
            pl.BlockSpec((D, tn), lambda i, j: (0, j)),
            pl.BlockSpec((D, LANES), lambda i, j: (0, 0)),
        ],
        out_specs=[
            pl.BlockSpec((tm, tn), lambda i, j: (i, j)),
            pl.BlockSpec((tm, LANES), lambda i, j: (i, 0)),
        ],
        out_shape=[
            jax.ShapeDtypeStruct((T, N), BF16),
            jax.ShapeDtypeStruct((T, LANES), F32),
        ],
        scratch_shapes=[pltpu.VMEM((tm, D), BF16)],
        compiler_params=_cparams("parallel", "arbitrary"),
        name="inproj_dn",
    )(x2d, lnw, w_dn, w_ba)


def _inproj_att(x2d, lnw, w_att, cos_t, sin_t, tm, tn):
    T, D = x2d.shape
    N = w_att.shape[1]
    S = cos_t.shape[0]
    body = functools.partial(_inproj_att_body, n_q_blocks=DF_WIDTH // tn, n_rope_blocks=2 * DF_WIDTH // tn,
                             q_scale=HEAD_DIM ** -0.5 * math.log2(math.e))
    return pl.pallas_call(
        body,
        grid=(T // tm, N // tn),
        in_specs=[
            pl.BlockSpec((tm, D), lambda i, j: (i, 0)),
            pl.BlockSpec((1, D), lambda i, j: (0, 0)),
            pl.BlockSpec((D, tn), lambda i, j: (0, j)),
            pl.BlockSpec((tm, HEAD_DIM), lambda i, j: (i % (S // tm), 0)),
            pl.BlockSpec((tm, HEAD_DIM), lambda i, j: (i % (S // tm), 0)),
        ],
        out_specs=pl.BlockSpec((tm, tn), lambda i, j: (i, j)),
        out_shape=jax.ShapeDtypeStruct((T, N), BF16),
        scratch_shapes=[pltpu.VMEM((tm, D), BF16)],
        compiler_params=_cparams("parallel", "arbitrary"),
        name="inproj_att",
    )(x2d, lnw, w_att, cos_t, sin_t)


CONV_HALO = 16
CONV_ROWS = 64


def _conv_body(x_ref, p_ref, w_ref, o_ref, *, n_q_blocks, n_qk_blocks, heads_per_block, q_scale):
    s = pl.program_id(1)
    c = pl.program_id(2)
    ts = x_ref.shape[1]
    w = w_ref[...]

    def conv_silu(r0):
        if r0 == 0:
            prev = jnp.where(s > 0, p_ref[0].astype(F32), 0.0)
        else:
            prev = x_ref[0, r0 - CONV_HALO:r0, :].astype(F32)
        cur = x_ref[0, r0:r0 + CONV_ROWS, :].astype(F32)
        xa = jnp.concatenate([prev, cur], axis=0)
        y = w[CONV_K - 1:CONV_K, :] * cur
        for d in range(1, CONV_K):
            y = y + w[CONV_K - 1 - d:CONV_K - d, :] * pltpu.roll(xa, d, axis=0)[CONV_HALO:, :]
        return _silu(y)

    def l2normed(y, scale):
        parts = []
        for hh in range(heads_per_block):
            yh = y[:, hh * HEAD_DIM:(hh + 1) * HEAD_DIM]
            ss = jnp.sum(yh * yh, axis=-1, keepdims=True)
            parts.append(yh * (lax.rsqrt(ss + EPS) * scale))
        return jnp.concatenate(parts, axis=-1)

    def run(post):
        for r0 in range(0, ts, CONV_ROWS):
            o_ref[0, r0:r0 + CONV_ROWS, :] = post(conv_silu(r0))

    @pl.when(c < n_q_blocks)
    def _():
        run(lambda y: l2normed(y, q_scale))

    @pl.when(jnp.logical_and(c >= n_q_blocks, c < n_qk_blocks))
    def _():
        run(lambda y: l2normed(y, 1.0))

    @pl.when(c >= n_qk_blocks)
    def _():
        run(lambda y: y)


def _dn_prep(proj3, conv_w, ts, ct):
    B, S, _ = proj3.shape
    C = conv_w.shape[1]
    body = functools.partial(
        _conv_body,
        n_q_blocks=DN_WIDTH // ct,
        n_qk_blocks=2 * DN_WIDTH // ct,
        heads_per_block=ct // HEAD_DIM,
        q_scale=HEAD_DIM ** -0.5,
    )
    return pl.pallas_call(
        body,
        grid=(B, S // ts, C // ct),
        in_specs=[
            pl.BlockSpec((1, ts, ct), lambda b, s, c: (b, s, c)),
            pl.BlockSpec((1, CONV_HALO, ct), lambda b, s, c: (b, jnp.maximum(s * (ts // CONV_HALO) - 1, 0), c)),
            pl.BlockSpec((CONV_K, ct), lambda b, s, c: (0, c)),
        ],
        out_specs=pl.BlockSpec((1, ts, ct), lambda b, s, c: (b, s, c)),
        out_shape=jax.ShapeDtypeStruct((B, S, C), F32),
        compiler_params=_cparams("parallel", "parallel", "parallel"),
        name="dn_prep",
    )(proj3, proj3, conv_w)


def _bdot(a, b):
    return jnp.dot(a.astype(BF16), b.astype(BF16), preferred_element_type=F32)


def _bdot_nt(a, b):
    return lax.dot_general(a.astype(BF16), b.astype(BF16), (((1,), (1,)), ((), ())),
                           preferred_element_type=F32)


def _bdot_tn(a, b):
    return lax.dot_general(a.astype(BF16), b.astype(BF16), (((0,), (0,)), ((), ())),
                           preferred_element_type=F32)


def _gdn_body(alog_ref, dtb_ref, da_ref, db_ref, q_ref, k_ref, v_ref, z_ref, nw_ref,
              o_ref, gc_sc, beta_sc, state_sc, lhs_sc, n_sc, r_sc, a_sc, *, n_chunks, n_heads, chunk):
    s = pl.program_id(2)
    C = chunk

    @pl.when(s == 0)
    def _init():
        for g in range(n_heads):
            x = da_ref[0, g] + dtb_ref[g]
            softplus = jnp.maximum(x, 0.0) + jnp.log1p(jnp.exp(-jnp.abs(x)))
            gl = -jnp.exp(alog_ref[g]) * softplus
            row = lax.broadcasted_iota(jnp.int32, gl.shape, 0)
            k = 1
            while k < C:
                gl = gl + jnp.where(row >= k, pltpu.roll(gl, k, axis=0), 0.0)
                k *= 2
            gc_sc[g] = gl
            beta_sc[g] = 1.0 / (1.0 + jnp.exp(-db_ref[0, g]))
        state_sc[...] = jnp.zeros_like(state_sc)

    ii = lax.broadcasted_iota(jnp.int32, (C, C), 0)
    jj = lax.broadcasted_iota(jnp.int32, (C, C), 1)
    tril = ii >= jj
    strict = ii > jj
    eye = ii == jj
    lane = lax.broadcasted_iota(jnp.int32, gc_sc.shape[1:], 1)
    nw = nw_ref[...]

    chunks = range(n_chunks)
    D = HEAD_DIM

    def phase1(g):
        cols = slice(g * D, (g + 1) * D)
        gcol, bcol, decay, egc, kdec, a_last = [], [], [], [], [], []
        q, k, v, kb = [], [], [], []
        for c in chunks:
            rows = slice(c * C, (c + 1) * C)
            sel = lane == s * n_chunks + c
            gc_ = jnp.sum(jnp.where(sel, gc_sc[g], 0.0), axis=1, keepdims=True)
            bc_ = jnp.sum(jnp.where(sel, beta_sc[g], 0.0), axis=1, keepdims=True)
            grow = jnp.sum(jnp.where(eye, jnp.broadcast_to(gc_, (C, C)), 0.0), axis=0, keepdims=True)
            glast = gc_[C - 1:C, :]
            gcol.append(gc_)
            bcol.append(bc_)
            decay.append(jnp.where(tril, jnp.exp(jnp.minimum(gc_ - grow, 0.0)), 0.0))
            egc.append(jnp.exp(gc_))
            a_last.append(jnp.exp(glast))
            q.append(q_ref[0, rows, cols])
            k.append(k_ref[0, rows, cols])
            v.append(v_ref[0, rows, cols])
            kb.append(k[c] * bc_)
            kdec.append(k[c] * jnp.exp(glast - gc_))

        kq = [_bdot_nt(jnp.concatenate([kb[c], q[c]], axis=0), k[c]) for c in chunks]
        a_mat = [jnp.where(strict, kq[c][:C] * decay[c], 0.0) for c in chunks]
        attn = [jnp.where(tril, kq[c][C:] * decay[c], 0.0) for c in chunks]

        t_inv = [jnp.where(eye, 1.0, 0.0) - jnp.where((ii >> 1) == (jj >> 1), a_mat[c], 0.0) for c in chunks]
        lvl = 1
        while (2 << lvl) <= C:
            off = jnp.logical_and((ii >> (lvl + 1)) == (jj >> (lvl + 1)),
                                  ((ii >> lvl) & 1) > ((jj >> lvl) & 1))
            ta = [_bdot(t_inv[c], jnp.where(off, a_mat[c], 0.0)) for c in chunks]
            t_inv = [t_inv[c] - _bdot(ta[c], t_inv[c]) for c in chunks]
            lvl += 1

        wu = [_bdot(t_inv[c], jnp.concatenate([kb[c] * egc[c], v[c] * bcol[c]], axis=-1)) for c in chunks]
        aw = [_bdot(attn[c], wu[c]) for c in chunks]
        mn = [_bdot_tn(kdec[c], wu[c]) for c in chunks]
        for c in chunks:
            p = q[c] * egc[c] - aw[c][:, :D]
            lhs_sc[g, c] = jnp.concatenate([mn[c][:, :D], p], axis=0).astype(BF16)
            n_sc[g, c] = mn[c][:, D:]
            r_sc[g, c] = aw[c][:, D:]
            a_sc[g, c] = jnp.broadcast_to(a_last[c], (8, D))

    for g in range(n_heads):
        phase1(g)

    state = [state_sc[g] for g in range(n_heads)]
    for c in chunks:
        rows = slice(c * C, (c + 1) * C)
        ms_o = [_bdot(lhs_sc[g, c], state[g]) for g in range(n_heads)]
        for g in range(n_heads):
            cols = slice(g * D, (g + 1) * D)
            state[g] = state[g] * a_sc[g, c][0:1, :] - ms_o[g][:D] + n_sc[g, c]
            o = ms_o[g][D:] + r_sc[g, c]
            ms = jnp.mean(o * o, axis=-1, keepdims=True)
            z = z_ref[0, rows, cols].astype(F32)
            o_ref[0, rows, cols] = ((o * lax.rsqrt(ms + EPS)) * nw * _silu(z)).astype(o_ref.dtype)
    for g in range(n_heads):
        state_sc[g] = state[g]


def _gdn(alog_b, dtb_b, da_c, db_c, qkv, proj3, nw, ts, n_heads, chunk):
    B, S, _ = qkv.shape
    G = n_heads
    HG = DN_HEADS // G
    n_chunks = ts // chunk
    n_seq_chunks = S // chunk
    wblk = G * HEAD_DIM
    z_off = 3 * DN_WIDTH // wblk
    body = functools.partial(_gdn_body, n_chunks=n_chunks, n_heads=G, chunk=chunk)
    return pl.pallas_call(
        body,
        grid=(B, HG, S // ts),
        in_specs=[
            pl.BlockSpec((G, 1, n_seq_chunks), lambda b, h, s: (h, 0, 0)),
            pl.BlockSpec((G, 1, n_seq_chunks), lambda b, h, s: (h, 0, 0)),
            pl.BlockSpec((1, G, chunk, n_seq_chunks), lambda b, h, s: (b, h, 0, 0)),
            pl.BlockSpec((1, G, chunk, n_seq_chunks), lambda b, h, s: (b, h, 0, 0)),
            pl.BlockSpec((1, ts, wblk), lambda b, h, s: (b, s, h)),
            pl.BlockSpec((1, ts, wblk), lambda b, h, s: (b, s, HG + h)),
            pl.BlockSpec((1, ts, wblk), lambda b, h, s: (b, s, 2 * HG + h)),
            pl.BlockSpec((1, ts, wblk), lambda b, h, s: (b, s, z_off + h)),
            pl.BlockSpec((1, HEAD_DIM), lambda b, h, s: (0, 0)),
        ],
        out_specs=pl.BlockSpec((1, ts, wblk), lambda b, h, s: (b, s, h)),
        out_shape=jax.ShapeDtypeStruct((B, S, DN_WIDTH), BF16),
        scratch_shapes=[
            pltpu.VMEM((G, chunk, n_seq_chunks), F32),
            pltpu.VMEM((G, chunk, n_seq_chunks), F32),
            pltpu.VMEM((G, HEAD_DIM, HEAD_DIM), F32),
            pltpu.VMEM((G, n_chunks, HEAD_DIM + chunk, HEAD_DIM), BF16),
            pltpu.VMEM((G, n_chunks, HEAD_DIM, HEAD_DIM), F32),
            pltpu.VMEM((G, n_chunks, chunk, HEAD_DIM), F32),
            pltpu.VMEM((G, n_chunks, 8, HEAD_DIM), F32),
        ],
        compiler_params=_cparams("parallel", "parallel", "arbitrary"),
        name="gdn",
    )(alog_b, dtb_b, da_c, db_c, qkv, qkv, qkv, proj3, nw)


def _rope_tables(S):
    pos = jnp.arange(S, dtype=F32)
    inv = ROPE_THETA ** (-jnp.arange(0, ROPE_DIM, 2, dtype=F32) / ROPE_DIM)
    ang = pos[:, None] * inv[None, :]
    c, s = jnp.cos(ang), jnp.sin(ang)
    rest = HEAD_DIM - ROPE_DIM
    cos = jnp.concatenate([c, c, jnp.ones((S, rest), F32)], axis=-1)
    sin = jnp.concatenate([-s, s, jnp.zeros((S, rest), F32)], axis=-1)
    return cos, sin


def _datt_body(lq1_ref, lk1_ref, lq2_ref, lk2_ref, nw_ref, q_ref, k_ref, v_ref, o_ref,
               m_sc, l_sc, acc_sc, s_sc, *, lambda_init, tq, tk):
    qi = pl.program_id(2)
    m_sc[...] = jnp.full_like(m_sc, -jnp.inf)
    l_sc[...] = jnp.zeros_like(l_sc)
    acc_sc[...] = jnp.zeros_like(acc_sc)

    def kv_rows(ki):
        return pl.ds(pl.multiple_of(ki * tk, tk), tk)

    def scores(ki, slot):
        for sub in range(2):
            sl = slice(sub * HEAD_DIM, (sub + 1) * HEAD_DIM)
            s_sc[slot, sub] = lax.dot_general(q_ref[0, :, sl], k_ref[0, kv_rows(ki), sl],
                                              (((1,), (1,)), ((), ())), preferred_element_type=F32)

    def softmax_pv(ki, slot, masked):
        v = v_ref[0, kv_rows(ki), :]
        for sub in range(2):
            s = s_sc[slot, sub]
            if masked:
                row = lax.broadcasted_iota(jnp.int32, s.shape, 0)
                col = lax.broadcasted_iota(jnp.int32, s.shape, 1)
                s = jnp.where(col <= row, s, -jnp.inf)
            m_prev = m_sc[sub]
            m_new = jnp.maximum(m_prev, jnp.max(s, axis=-1, keepdims=True))
            alpha = jnp.exp2(m_prev - m_new)
            p = jnp.exp2(s - _lane_tile(m_new, tk // LANES))
            l_sc[sub] = alpha * l_sc[sub] + jnp.sum(p, axis=-1, keepdims=True)
            acc_sc[sub] = (_lane_tile(alpha, DF_DV // LANES) * acc_sc[sub]
                           + jnp.dot(p.astype(BF16), v, preferred_element_type=F32))
            m_sc[sub] = m_new

    scores(0, 0)

    def pair(j, carry):
        k0 = 2 * j
        scores(k0 + 1, 1)
        softmax_pv(k0, 0, False)
        scores(k0 + 2, 0)
        softmax_pv(k0 + 1, 1, False)
        return carry

    lax.fori_loop(0, qi // 2, pair, 0)

    @pl.when(qi % 2 == 1)
    def _():
        scores(qi, 1)
        softmax_pv(qi - 1, 0, False)
        softmax_pv(qi, 1, True)

    @pl.when(qi % 2 == 0)
    def _():
        softmax_pv(qi, 0, True)

    lam = (jnp.exp(jnp.sum(lq1_ref[...] * lk1_ref[...], axis=-1, keepdims=True))
           - jnp.exp(jnp.sum(lq2_ref[...] * lk2_ref[...], axis=-1, keepdims=True))
           + lambda_init)
    reps = DF_DV // LANES
    o = (acc_sc[0] / _lane_tile(l_sc[0], reps)
         - lam * (acc_sc[1] / _lane_tile(l_sc[1], reps)))
    ms = jnp.mean(o * o, axis=-1, keepdims=True)
    o_ref[0] = ((o * lax.rsqrt(ms + EPS)) * nw_ref[...] * (1.0 - lambda_init)).astype(o_ref.dtype)


def _diff_attention(lq1, lk1, lq2, lk2, nw, qkv, lambda_init, tq):
    B, S, _ = qkv.shape
    H = DF_HEADS
    body = functools.partial(_datt_body, lambda_init=lambda_init, tq=tq, tk=tq)
    vec = pl.BlockSpec((1, HEAD_DIM), lambda b, h, qi: (0, 0))
    return pl.pallas_call(
        body,
        grid=(B, H, S // tq),
        in_specs=[
            vec, vec, vec, vec,
            pl.BlockSpec((1, DF_DV), lambda b, h, qi: (0, 0)),
            pl.BlockSpec((1, tq, DF_DV), lambda b, h, qi: (b, qi, h)),
            pl.BlockSpec((1, S, DF_DV), lambda b, h, qi: (b, 0, H + h)),
            pl.BlockSpec((1, S, DF_DV), lambda b, h, qi: (b, 0, 2 * H + h)),
        ],
        out_specs=pl.BlockSpec((1, tq, DF_DV), lambda b, h, qi: (b, qi, h)),
        out_shape=jax.ShapeDtypeStruct((B, S, DF_WIDTH), BF16),
        scratch_shapes=[
            pltpu.VMEM((2, tq, LANES), F32),
            pltpu.VMEM((2, tq, LANES), F32),
            pltpu.VMEM((2, tq, DF_DV), F32),
            pltpu.VMEM((2, 2, tq, tq), F32),
        ],
        compiler_params=_cparams("parallel", "parallel", "arbitrary"),
        name="diff_attn",
    )(lq1, lk1, lq2, lk2, nw, qkv, qkv, qkv)


def _outproj_body(a1_ref, a2_ref, w1_ref, w2_ref, r_ref, lnw_ref, o_ref, xg_ref, ss_ref):
    acc = jnp.dot(a1_ref[...], w1_ref[...], preferred_element_type=F32)
    acc = acc + jnp.dot(a2_ref[...], w2_ref[...], preferred_element_type=F32)
    x1 = r_ref[...] + acc
    o_ref[...] = x1
    xg_ref[...] = (x1 * lnw_ref[...]).astype(xg_ref.dtype)
    sq = x1 * x1
    part = sq[:, :LANES]
    for t in range(1, sq.shape[1] // LANES):
        part = part + sq[:, t * LANES:(t + 1) * LANES]

    @pl.when(pl.program_id(1) == 0)
    def _():
        ss_ref[...] = part

    @pl.when(pl.program_id(1) != 0)
    def _():
        ss_ref[...] = ss_ref[...] + part


def _outproj(a1, a2, w, resid, lnw, tm, tn):
    T, K1 = a1.shape
    K2 = a2.shape[1]
    N = w.shape[1]
    assert K1 == K2 and w.shape[0] == K1 + K2
    return pl.pallas_call(
        _outproj_body,
        grid=(T // tm, N // tn),
        in_specs=[
            pl.BlockSpec((tm, K1), lambda i, j: (i, 0)),
            pl.BlockSpec((tm, K2), lambda i, j: (i, 0)),
            pl.BlockSpec((K1, tn), lambda i, j: (0, j)),
            pl.BlockSpec((K2, tn), lambda i, j: (1, j)),
            pl.BlockSpec((tm, tn), lambda i, j: (i, j)),
            pl.BlockSpec((1, tn), lambda i, j: (0, j)),
        ],
        out_specs=[
            pl.BlockSpec((tm, tn), lambda i, j: (i, j)),
            pl.BlockSpec((tm, tn), lambda i, j: (i, j)),
            pl.BlockSpec((tm, LANES), lambda i, j: (i, 0)),
        ],
        out_shape=[
            jax.ShapeDtypeStruct((T, N), F32),
            jax.ShapeDtypeStruct((T, N), BF16),
            jax.ShapeDtypeStruct((T, LANES), F32),
        ],
        compiler_params=_cparams("parallel", "arbitrary"),
        name="outproj",
    )(a1, a2, w, w, resid, lnw)


def _ffn_up_body(xg_ref, ss_ref, wg_ref, wu_ref, o_ref):
    d_model = xg_ref.shape[1]
    r = lax.rsqrt(jnp.sum(ss_ref[...], axis=-1, keepdims=True) * (1.0 / d_model) + EPS)
    xg = xg_ref[...]
    g = jnp.dot(xg, wg_ref[...], preferred_element_type=F32) * r
    u = jnp.dot(xg, wu_ref[...], preferred_element_type=F32) * r
    o_ref[...] = (_silu(g) * u).astype(o_ref.dtype)


def _ffn_up(xg, ss, wg, wu, tm, tn):
    T, D = xg.shape
    N = wg.shape[1]
    return pl.pallas_call(
        _ffn_up_body,
        grid=(T // tm, N // tn),
        in_specs=[
            pl.BlockSpec((tm, D), lambda i, j: (i, 0)),
            pl.BlockSpec((tm, LANES), lambda i, j: (i, 0)),
            pl.BlockSpec((D, tn), lambda i, j: (0, j)),
            pl.BlockSpec((D, tn), lambda i, j: (0, j)),
        ],
        out_specs=pl.BlockSpec((tm, tn), lambda i, j: (i, j)),
        out_shape=jax.ShapeDtypeStruct((T, N), BF16),
        compiler_params=_cparams("parallel", "parallel"),
        name="ffn_up",
    )(xg, ss, wg, wu)


def _ffn_down_body(a_ref, w_ref, r_ref, o_ref):
    d = jnp.dot(a_ref[...], w_ref[...], preferred_element_type=F32)

    @pl.when(pl.program_id(2) == 0)
    def _():
        o_ref[...] = r_ref[...] + d

    @pl.when(pl.program_id(2) != 0)
    def _():
        o_ref[...] = o_ref[...] + d


def _ffn_down(a, w, resid, tm, tn, tk):
    T, K = a.shape
    N = w.shape[1]
    return pl.pallas_call(
        _ffn_down_body,
        grid=(T // tm, N // tn, K // tk),
        in_specs=[
            pl.BlockSpec((tm, tk), lambda i, j, k: (i, k)),
            pl.BlockSpec((tk, tn), lambda i, j, k: (k, j)),
            pl.BlockSpec((tm, tn), lambda i, j, k: (i, j)),
        ],
        out_specs=pl.BlockSpec((tm, tn), lambda i, j, k: (i, j)),
        out_shape=jax.ShapeDtypeStruct((T, N), F32),
        compiler_params=_cparams("parallel", "parallel", "arbitrary"),
        name="ffn_down",
    )(a, w, resid)


def _rmsnorm_body(x_ref, w_ref, o_ref):
    x = x_ref[...]
    ms = jnp.mean(x * x, axis=-1, keepdims=True)
    o_ref[...] = x * lax.rsqrt(ms + EPS) * w_ref[...]


def _rmsnorm(x2d, w, tm):
    T, D = x2d.shape
    return pl.pallas_call(
        _rmsnorm_body,
        grid=(T // tm,),
        in_specs=[pl.BlockSpec((tm, D), lambda i: (i, 0)), pl.BlockSpec((1, D), lambda i: (0, 0))],
        out_specs=pl.BlockSpec((tm, D), lambda i: (i, 0)),
        out_shape=jax.ShapeDtypeStruct((T, D), F32),
        compiler_params=_cparams("parallel"),
        name="final_norm",
    )(x2d, w)


def _layer(x, l, ln_mix_w, w_in, conv_w, a_log, dt_bias, dn_norm_w, lambda_q1, lambda_k1, lambda_q2,
           lambda_k2, df_norm_w, w_out, ln_ffn_w, w_gate, w_up, w_down, cos_t, sin_t):
    B, S, D = x.shape
    T = B * S
    x2d = x.reshape(T, D)

    n_a = 4 * DN_WIDTH
    n_ba = 2 * DN_HEADS
    w_dn = w_in[:, :n_a].astype(BF16)
    w_att = w_in[:, n_a + n_ba:].astype(BF16)
    w_ba = jnp.pad(w_in[:, n_a:n_a + n_ba], ((0, 0), (0, LANES - n_ba))).astype(BF16)

    tm_in = min(512, S)
    lnw = ln_mix_w.reshape(1, D)
    proj_dn, ba = _inproj_dn(x2d, lnw, w_dn, w_ba, tm=tm_in, tn=1024)
    qkv_att = _inproj_att(x2d, lnw, w_att, cos_t, sin_t, tm=tm_in, tn=1024).reshape(B, S, -1)
    proj3 = proj_dn.reshape(B, S, -1)

    qkv_dn = _dn_prep(proj3, conv_w, ts=min(512, S), ct=1024)
    n_sc = S // GDN_CHUNK
    ba4 = ba.reshape(B, n_sc, GDN_CHUNK, LANES)
    db_c = ba4[..., :DN_HEADS].transpose(0, 3, 2, 1)
    da_c = ba4[..., DN_HEADS:n_ba].transpose(0, 3, 2, 1)
    alog_b = jnp.broadcast_to(a_log.astype(F32)[:, None, None], (DN_HEADS, 1, n_sc))
    dtb_b = jnp.broadcast_to(dt_bias.astype(F32)[:, None, None], (DN_HEADS, 1, n_sc))
    o_dn = _gdn(alog_b, dtb_b, da_c, db_c, qkv_dn, proj3, dn_norm_w.reshape(1, HEAD_DIM),
                ts=min(GDN_TILE, S), n_heads=GDN_HEADS_PER_STEP, chunk=GDN_CHUNK)

    lambda_init = 0.8 - 0.6 * math.exp(-0.3 * l)
    o_df = _diff_attention(lambda_q1.reshape(1, -1), lambda_k1.reshape(1, -1), lambda_q2.reshape(1, -1),
                           lambda_k2.reshape(1, -1), df_norm_w.reshape(1, -1), qkv_att, lambda_init,
                           tq=min(512, S))

    x1, x1g, x1ss = _outproj(o_dn.reshape(T, -1), o_df.reshape(T, -1), w_out.astype(BF16), x2d,
                             ln_ffn_w.reshape(1, D), tm=min(1024, T), tn=512)

    act = _ffn_up(x1g, x1ss, w_gate.astype(BF16), w_up.astype(BF16), tm=min(FFN_UP_ROWS, T), tn=256)
    x2 = _ffn_down(act, w_down.astype(BF16), x1, tm=min(1024, T), tn=512, tk=w_down.shape[0] // 2)
    return x2.reshape(B, S, D)


def kernel(x, ln_mix_w, w_in, conv_w, a_log, dt_bias, dn_norm_w, lambda_q1, lambda_k1, lambda_q2, lambda_k2,
           df_norm_w, w_out, ln_ffn_w, w_gate, w_up, w_down, ln_final_w):
    B, S, D = x.shape
    cos_t, sin_t = _rope_tables(S)
    depth = w_in.shape[0]
    for l in range(depth):
        x = _layer(x, l, ln_mix_w[l], w_in[l], conv_w[l], a_log[l], dt_bias[l], dn_norm_w[l], lambda_q1[l],
                   lambda_k1[l], lambda_q2[l], lambda_k2[l], df_norm_w[l], w_out[l], ln_ffn_w[l], w_gate[l],
                   w_up[l], w_down[l], cos_t, sin_t)
    out = _rmsnorm(x.reshape(B * S, D), ln_final_w.reshape(1, D), tm=min(256, B * S))
    return out.reshape(B, S, D)
```

```python
import functools
import math

import jax
import jax.numpy as jnp
from jax import lax
from jax.experimental import pallas as pl
from jax.experimental.pallas import tpu as pltpu

F32 = jnp.float32
BF16 = jnp.bfloat16

EPS = 1e-6
HEAD_DIM = 128
DN_HEADS = 16
DN_WIDTH = DN_HEADS * HEAD_DIM
CONV_K = 4
GDN_CHUNK = 128
GDN_TILE = 1024
GDN_HEADS_PER_STEP = 4
FFN_UP_ROWS = 2048
DF_HEADS = 8
DF_DV = 2 * HEAD_DIM
DF_WIDTH = DF_HEADS * DF_DV
ROPE_THETA = 500000.0
ROPE_DIM = HEAD_DIM // 4
LANES = 128
VMEM_LIMIT = 56 * 1024 * 1024


def _cparams(*sem):
    return pltpu.CompilerParams(dimension_semantics=sem, vmem_limit_bytes=VMEM_LIMIT)


def _silu(x):
    return x / (1.0 + jnp.exp(-x))


def _lane_tile(x, n):
    return x if n == 1 else jnp.concatenate([x] * n, axis=1)


def _lane_partial_sums(v):
    part = v[:, :LANES]
    for t in range(1, v.shape[1] // LANES):
        part = part + v[:, t * LANES:(t + 1) * LANES]
    return part


def _row_factor(ss_ref, d_model):
    return lax.rsqrt(jnp.sum(ss_ref[...], axis=-1, keepdims=True) * (1.0 / d_model) + EPS)


def _rms_bf16_body(x_ref, w_ref, o_ref):
    x = x_ref[...]
    ms = jnp.mean(x * x, axis=-1, keepdims=True)
    o_ref[...] = (x * lax.rsqrt(ms + EPS) * w_ref[...]).astype(o_ref.dtype)


def _rms_bf16(x2d, lnw, tm):
    T, D = x2d.shape
    return pl.pallas_call(
        _rms_bf16_body,
        grid=(T // tm,),
        in_specs=[pl.BlockSpec((tm, D), lambda i: (i, 0)), pl.BlockSpec((1, D), lambda i: (0, 0))],
        out_specs=pl.BlockSpec((tm, D), lambda i: (i, 0)),
        out_shape=jax.ShapeDtypeStruct((T, D), BF16),
        compiler_params=_cparams("parallel"),
        name="mix_norm",
    )(x2d, lnw)


def _inproj_dn_body(h_ref, w_ref, wba_ref, o_ref, oba_ref):
    h = h_ref[...]

    @pl.when(pl.program_id(1) == 0)
    def _():
        oba_ref[...] = jnp.dot(h, wba_ref[...], preferred_element_type=F32)

    o_ref[...] = jnp.dot(h, w_ref[...], preferred_element_type=F32).astype(o_ref.dtype)


def _inproj_att_body(h_ref, w_ref, cos_ref, sin_ref, o_ref):
    acc = jnp.dot(h_ref[...], w_ref[...], preferred_element_type=F32)
    cos = cos_ref[0]
    sin = sin_ref[0]
    lane = lax.broadcasted_iota(jnp.int32, cos.shape, 1)
    half = ROPE_DIM // 2
    for hh in range(acc.shape[1] // HEAD_DIM):
        sl = slice(hh * HEAD_DIM, (hh + 1) * HEAD_DIM)
        xh = acc[:, sl]
        partner = jnp.where(lane < half,
                            pltpu.roll(xh, HEAD_DIM - half, axis=1),
                            pltpu.roll(xh, half, axis=1))
        o_ref[:, sl] = (xh * cos + partner * sin).astype(o_ref.dtype)


def _inproj_dn(h, w_in_bf, w_ba, n_cols, tm, tn):
    T, D = h.shape
    return pl.pallas_call(
        _inproj_dn_body,
        grid=(T // tm, n_cols // tn),
        in_specs=[
            pl.BlockSpec((tm, D), lambda i, j: (i, 0)),
            pl.BlockSpec((D, tn), lambda i, j: (0, j)),
            pl.BlockSpec((D, LANES), lambda i, j: (0, 0)),
        ],
        out_specs=[
            pl.BlockSpec((tm, tn), lambda i, j: (i, j)),
            pl.BlockSpec((tm, LANES), lambda i, j: (i, 0)),
        ],
        out_shape=[
            jax.ShapeDtypeStruct((T, n_cols), BF16),
            jax.ShapeDtypeStruct((T, LANES), F32),
        ],
        compiler_params=_cparams("parallel", "arbitrary"),
        name="inproj_dn",
    )(h, w_in_bf, w_ba)


def _inproj_att(h, w_att, cos_t, sin_t, tm, tn):
    T, D = h.shape
    N = w_att.shape[1]
    S = cos_t.shape[1]
    per_kind = DF_WIDTH // tn
    table = pl.BlockSpec((1, tm, HEAD_DIM), lambda i, j: (j // per_kind, i % (S // tm), 0))
    return pl.pallas_call(
        _inproj_att_body,
        grid=(T // tm, N // tn),
        in_specs=[
            pl.BlockSpec((tm, D), lambda i, j: (i, 0)),
            pl.BlockSpec((D, tn), lambda i, j: (0, j)),
            table,
            table,
        ],
        out_specs=pl.BlockSpec((tm, tn), lambda i, j: (i, j)),
        out_shape=jax.ShapeDtypeStruct((T, N), BF16),
        compiler_params=_cparams("parallel", "parallel"),
        name="inproj_att",
    )(h, w_att, cos_t, sin_t)


CONV_HALO = 16
CONV_ROWS = 64


def _conv_body(x_ref, p_ref, w_ref, o_ref, *, n_q_blocks, n_qk_blocks, heads_per_block, q_scale):
    s = pl.program_id(1)
    c = pl.program_id(2)
    ts = x_ref.shape[1]
    w = w_ref[...]

    def conv_silu(r0):
        if r0 == 0:
            prev = jnp.where(s > 0, p_ref[0].astype(F32), 0.0)
        else:
            prev = x_ref[0, r0 - CONV_HALO:r0, :].astype(F32)
        cur = x_ref[0, r0:r0 + CONV_ROWS, :].astype(F32)
        xa = jnp.concatenate([prev, cur], axis=0)
        y = w[CONV_K - 1:CONV_K, :] * cur
        for d in range(1, CONV_K):
            y = y + w[CONV_K - 1 - d:CONV_K - d, :] * pltpu.roll(xa, d, axis=0)[CONV_HALO:, :]
        return _silu(y)

    def l2normed(y, scale):
        parts = []
        for hh in range(heads_per_block):
            yh = y[:, hh * HEAD_DIM:(hh + 1) * HEAD_DIM]
            ss = jnp.sum(yh * yh, axis=-1, keepdims=True)
            parts.append(yh * (lax.rsqrt(ss + EPS) * scale))
        return jnp.concatenate(parts, axis=-1)

    def run(post):
        for r0 in range(0, ts, CONV_ROWS):
            o_ref[0, r0:r0 + CONV_ROWS, :] = post(conv_silu(r0))

    @pl.when(c < n_q_blocks)
    def _():
        run(lambda y: l2normed(y, q_scale))

    @pl.when(jnp.logical_and(c >= n_q_blocks, c < n_qk_blocks))
    def _():
        run(lambda y: l2normed(y, 1.0))

    @pl.when(c >= n_qk_blocks)
    def _():
        run(lambda y: y)


def _dn_prep(proj3, conv_w, ts, ct):
    B, S, _ = proj3.shape
    C = conv_w.shape[1]
    body = functools.partial(
        _conv_body,
        n_q_blocks=DN_WIDTH // ct,
        n_qk_blocks=2 * DN_WIDTH // ct,
        heads_per_block=ct // HEAD_DIM,
        q_scale=HEAD_DIM ** -0.5,
    )
    return pl.pallas_call(
        body,
        grid=(B, S // ts, C // ct),
        in_specs=[
            pl.BlockSpec((1, ts, ct), lambda b, s, c: (b, s, c)),
            pl.BlockSpec((1, CONV_HALO, ct), lambda b, s, c: (b, jnp.maximum(s * (ts // CONV_HALO) - 1, 0), c)),
            pl.BlockSpec((CONV_K, ct), lambda b, s, c: (0, c)),
        ],
        out_specs=pl.BlockSpec((1, ts, ct), lambda b, s, c: (b, s, c)),
        out_shape=jax.ShapeDtypeStruct((B, S, C), F32),
        compiler_params=_cparams("parallel", "parallel", "parallel"),
        name="dn_prep",
    )(proj3, proj3, conv_w)


def _bdot(a, b):
    return jnp.dot(a.astype(BF16), b.astype(BF16), preferred_element_type=F32)


def _bdot_nt(a, b):
    return lax.dot_general(a.astype(BF16), b.astype(BF16), (((1,), (1,)), ((), ())),
                           preferred_element_type=F32)


def _bdot_tn(a, b):
    return lax.dot_general(a.astype(BF16), b.astype(BF16), (((0,), (0,)), ((), ())),
                           preferred_element_type=F32)


def _gdn_body(alog_ref, dtb_ref, da_ref, db_ref, q_ref, k_ref, v_ref, z_ref, nw_ref,
              o_ref, gc_sc, beta_sc, state_sc, lhs_sc, n_sc, r_sc, a_sc, *, n_chunks, n_heads, chunk):
    s = pl.program_id(2)
    C = chunk

    @pl.when(s == 0)
    def _init():
        for g in range(n_heads):
            x = da_ref[0, g] + dtb_ref[g]
            softplus = jnp.maximum(x, 0.0) + jnp.log1p(jnp.exp(-jnp.abs(x)))
            gl = -jnp.exp(alog_ref[g]) * softplus
            row = lax.broadcasted_iota(jnp.int32, gl.shape, 0)
            k = 1
            while k < C:
                gl = gl + jnp.where(row >= k, pltpu.roll(gl, k, axis=0), 0.0)
                k *= 2
            gc_sc[g] = gl
            beta_sc[g] = 1.0 / (1.0 + jnp.exp(-db_ref[0, g]))
        state_sc[...] = jnp.zeros_like(state_sc)

    ii = lax.broadcasted_iota(jnp.int32, (C, C), 0)
    jj = lax.broadcasted_iota(jnp.int32, (C, C), 1)
    tril = ii >= jj
    strict = ii > jj
    eye = ii == jj
    lane = lax.broadcasted_iota(jnp.int32, gc_sc.shape[1:], 1)
    nw = nw_ref[...]

    chunks = range(n_chunks)
    D = HEAD_DIM

    def phase1(g):
        cols = slice(g * D, (g + 1) * D)
        gcol, bcol, decay, egc, kdec, a_last = [], [], [], [], [], []
        q, k, v, kb = [], [], [], []
        for c in chunks:
            rows = slice(c * C, (c + 1) * C)
            sel = lane == s * n_chunks + c
            gc_ = jnp.sum(jnp.where(sel, gc_sc[g], 0.0), axis=1, keepdims=True)
            bc_ = jnp.sum(jnp.where(sel, beta_sc[g], 0.0), axis=1, keepdims=True)
            grow = jnp.sum(jnp.where(eye, jnp.broadcast_to(gc_, (C, C)), 0.0), axis=0, keepdims=True)
            glast = gc_[C - 1:C, :]
            gcol.append(gc_)
            bcol.append(bc_)
            decay.append(jnp.where(tril, jnp.exp(jnp.minimum(gc_ - grow, 0.0)), 0.0))
            egc.append(jnp.exp(gc_))
            a_last.append(jnp.exp(glast))
            q.append(q_ref[0, rows, cols])
            k.append(k_ref[0, rows, cols])
            v.append(v_ref[0, rows, cols])
            kb.append(k[c] * bc_)
            kdec.append(k[c] * jnp.exp(glast - gc_))

        kq = [_bdot_nt(jnp.concatenate([kb[c], q[c]], axis=0), k[c]) for c in chunks]
        a_mat = [jnp.where(strict, kq[c][:C] * decay[c], 0.0) for c in chunks]
        attn = [jnp.where(tril, kq[c][C:] * decay[c], 0.0) for c in chunks]

        t_inv = [jnp.where(eye, 1.0, 0.0) - jnp.where((ii >> 1) == (jj >> 1), a_mat[c], 0.0) for c in chunks]
        lvl = 1
        while (2 << lvl) <= C:
            off = jnp.logical_and((ii >> (lvl + 1)) == (jj >> (lvl + 1)),
                                  ((ii >> lvl) & 1) > ((jj >> lvl) & 1))
            ta = [_bdot(t_inv[c], jnp.where(off, a_mat[c], 0.0)) for c in chunks]
            t_inv = [t_inv[c] - _bdot(ta[c], t_inv[c]) for c in chunks]
            lvl += 1

        wu = [_bdot(t_inv[c], jnp.concatenate([kb[c] * egc[c], v[c] * bcol[c]], axis=-1)) for c in chunks]
        aw = [_bdot(attn[c], wu[c]) for c in chunks]
        mn = [_bdot_tn(kdec[c], wu[c]) for c in chunks]
        for c in chunks:
            p = q[c] * egc[c] - aw[c][:, :D]
            lhs_sc[g, c] = jnp.concatenate([mn[c][:, :D], p], axis=0).astype(BF16)
            n_sc[g, c] = mn[c][:, D:]
            r_sc[g, c] = aw[c][:, D:]
            a_sc[g, c] = jnp.broadcast_to(a_last[c], (8, D))

    for g in range(n_heads):
        phase1(g)

    state = [state_sc[g] for g in range(n_heads)]
    for c in chunks:
        rows = slice(c * C, (c + 1) * C)
        ms_o = [_bdot(lhs_sc[g, c], state[g]) for g in range(n_heads)]
        for g in range(n_heads):
            cols = slice(g * D, (g + 1) * D)
            state[g] = state[g] * a_sc[g, c][0:1, :] - ms_o[g][:D] + n_sc[g, c]
            o = ms_o[g][D:] + r_sc[g, c]
            ms = jnp.mean(o * o, axis=-1, keepdims=True)
            z = z_ref[0, rows, cols].astype(F32)
            o_ref[0, rows, cols] = ((o * lax.rsqrt(ms + EPS)) * nw * _silu(z)).astype(o_ref.dtype)
    for g in range(n_heads):
        state_sc[g] = state[g]


def _gdn(alog_b, dtb_b, da_c, db_c, qkv, proj3, nw, ts, n_heads, chunk):
    B, S, _ = qkv.shape
    G = n_heads
    HG = DN_HEADS // G
    n_chunks = ts // chunk
    n_seq_chunks = S // chunk
    wblk = G * HEAD_DIM
    z_off = 3 * DN_WIDTH // wblk
    body = functools.partial(_gdn_body, n_chunks=n_chunks, n_heads=G, chunk=chunk)
    return pl.pallas_call(
        body,
        grid=(B, HG, S // ts),
        in_specs=[
            pl.BlockSpec((G, 1, n_seq_chunks), lambda b, h, s: (h, 0, 0)),
            pl.BlockSpec((G, 1, n_seq_chunks), lambda b, h, s: (h, 0, 0)),
            pl.BlockSpec((1, G, chunk, n_seq_chunks), lambda b, h, s: (b, h, 0, 0)),
            pl.BlockSpec((1, G, chunk, n_seq_chunks), lambda b, h, s: (b, h, 0, 0)),
            pl.BlockSpec((1, ts, wblk), lambda b, h, s: (b, s, h)),
            pl.BlockSpec((1, ts, wblk), lambda b, h, s: (b, s, HG + h)),
            pl.BlockSpec((1, ts, wblk), lambda b, h, s: (b, s, 2 * HG + h)),
            pl.BlockSpec((1, ts, wblk), lambda b, h, s: (b, s, z_off + h)),
            pl.BlockSpec((1, HEAD_DIM), lambda b, h, s: (0, 0)),
        ],
        out_specs=pl.BlockSpec((1, ts, wblk), lambda b, h, s: (b, s, h)),
        out_shape=jax.ShapeDtypeStruct((B, S, DN_WIDTH), BF16),
        scratch_shapes=[
            pltpu.VMEM((G, chunk, n_seq_chunks), F32),
            pltpu.VMEM((G, chunk, n_seq_chunks), F32),
            pltpu.VMEM((G, HEAD_DIM, HEAD_DIM), F32),
            pltpu.VMEM((G, n_chunks, HEAD_DIM + chunk, HEAD_DIM), BF16),
            pltpu.VMEM((G, n_chunks, HEAD_DIM, HEAD_DIM), F32),
            pltpu.VMEM((G, n_chunks, chunk, HEAD_DIM), F32),
            pltpu.VMEM((G, n_chunks, 8, HEAD_DIM), F32),
        ],
        compiler_params=_cparams("parallel", "parallel", "arbitrary"),
        name="gdn",
    )(alog_b, dtb_b, da_c, db_c, qkv, qkv, qkv, proj3, nw)


def _rope_tables(S):
    pos = jnp.arange(S, dtype=F32)
    inv = ROPE_THETA ** (-jnp.arange(0, ROPE_DIM, 2, dtype=F32) / ROPE_DIM)
    ang = pos[:, None] * inv[None, :]
    c, s = jnp.cos(ang), jnp.sin(ang)
    rest = HEAD_DIM - ROPE_DIM
    cos = jnp.concatenate([c, c, jnp.ones((S, rest), F32)], axis=-1)
    sin = jnp.concatenate([-s, s, jnp.zeros((S, rest), F32)], axis=-1)
    q_scale = HEAD_DIM ** -0.5 * math.log2(math.e)
    cos3 = jnp.stack([cos * q_scale, cos, jnp.ones_like(cos)])
    sin3 = jnp.stack([sin * q_scale, sin, jnp.zeros_like(sin)])
    return cos3, sin3


def _datt_body(lq1_ref, lk1_ref, lq2_ref, lk2_ref, nw_ref, q_ref, k_ref, v_ref, o_ref,
               m_sc, l_sc, acc_sc, s_sc, *, lambda_init, tq, tk):
    qi = pl.program_id(2)
    m_sc[...] = jnp.full_like(m_sc, -jnp.inf)
    l_sc[...] = jnp.zeros_like(l_sc)
    acc_sc[...] = jnp.zeros_like(acc_sc)

    def kv_rows(ki):
        return pl.ds(pl.multiple_of(ki * tk, tk), tk)

    def scores(ki, slot):
        for sub in range(2):
            sl = slice(sub * HEAD_DIM, (sub + 1) * HEAD_DIM)
            s_sc[slot, sub] = lax.dot_general(q_ref[0, :, sl], k_ref[0, kv_rows(ki), sl],
                                              (((1,), (1,)), ((), ())), preferred_element_type=F32)

    def softmax_pv(ki, slot, masked):
        v = v_ref[0, kv_rows(ki), :]
        for sub in range(2):
            s = s_sc[slot, sub]
            if masked:
                row = lax.broadcasted_iota(jnp.int32, s.shape, 0)
                col = lax.broadcasted_iota(jnp.int32, s.shape, 1)
                s = jnp.where(col <= row, s, -jnp.inf)
            m_prev = m_sc[sub]
            m_new = jnp.maximum(m_prev, jnp.max(s, axis=-1, keepdims=True))
            alpha = jnp.exp2(m_prev - m_new)
            p = jnp.exp2(s - _lane_tile(m_new, tk // LANES))
            l_sc[sub] = alpha * l_sc[sub] + jnp.sum(p, axis=-1, keepdims=True)
            acc_sc[sub] = (_lane_tile(alpha, DF_DV // LANES) * acc_sc[sub]
                           + jnp.dot(p.astype(BF16), v, preferred_element_type=F32))
            m_sc[sub] = m_new

    scores(0, 0)

    def pair(j, carry):
        k0 = 2 * j
        scores(k0 + 1, 1)
        softmax_pv(k0, 0, False)
        scores(k0 + 2, 0)
        softmax_pv(k0 + 1, 1, False)
        return carry

    lax.fori_loop(0, qi // 2, pair, 0)

    @pl.when(qi % 2 == 1)
    def _():
        scores(qi, 1)
        softmax_pv(qi - 1, 0, False)
        softmax_pv(qi, 1, True)

    @pl.when(qi % 2 == 0)
    def _():
        softmax_pv(qi, 0, True)

    lam = (jnp.exp(jnp.sum(lq1_ref[...] * lk1_ref[...], axis=-1, keepdims=True))
           - jnp.exp(jnp.sum(lq2_ref[...] * lk2_ref[...], axis=-1, keepdims=True))
           + lambda_init)
    reps = DF_DV // LANES
    o = (acc_sc[0] / _lane_tile(l_sc[0], reps)
         - lam * (acc_sc[1] / _lane_tile(l_sc[1], reps)))
    ms = jnp.mean(o * o, axis=-1, keepdims=True)
    o_ref[0] = ((o * lax.rsqrt(ms + EPS)) * nw_ref[...] * (1.0 - lambda_init)).astype(o_ref.dtype)


def _diff_attention(lq1, lk1, lq2, lk2, nw, qkv, lambda_init, tq):
    B, S, _ = qkv.shape
    H = DF_HEADS
    body = functools.partial(_datt_body, lambda_init=lambda_init, tq=tq, tk=tq)
    vec = pl.BlockSpec((1, HEAD_DIM), lambda b, h, qi: (0, 0))
    return pl.pallas_call(
        body,
        grid=(B, H, S // tq),
        in_specs=[
            vec, vec, vec, vec,
            pl.BlockSpec((1, DF_DV), lambda b, h, qi: (0, 0)),
            pl.BlockSpec((1, tq, DF_DV), lambda b, h, qi: (b, qi, h)),
            pl.BlockSpec((1, S, DF_DV), lambda b, h, qi: (b, 0, H + h)),
            pl.BlockSpec((1, S, DF_DV), lambda b, h, qi: (b, 0, 2 * H + h)),
        ],
        out_specs=pl.BlockSpec((1, tq, DF_DV), lambda b, h, qi: (b, qi, h)),
        out_shape=jax.ShapeDtypeStruct((B, S, DF_WIDTH), BF16),
        scratch_shapes=[
            pltpu.VMEM((2, tq, LANES), F32),
            pltpu.VMEM((2, tq, LANES), F32),
            pltpu.VMEM((2, tq, DF_DV), F32),
            pltpu.VMEM((2, 2, tq, tq), F32),
        ],
        compiler_params=_cparams("parallel", "parallel", "arbitrary"),
        name="diff_attn",
    )(lq1, lk1, lq2, lk2, nw, qkv, qkv, qkv)


def _outproj_body(a1_ref, a2_ref, w1_ref, w2_ref, r_ref, lnw_ref, o_ref, xg_ref, ss_ref):
    acc = jnp.dot(a1_ref[...], w1_ref[...], preferred_element_type=F32)
    acc = acc + jnp.dot(a2_ref[...], w2_ref[...], preferred_element_type=F32)
    x1 = r_ref[...] + acc
    o_ref[...] = x1
    xg_ref[...] = (x1 * lnw_ref[...]).astype(xg_ref.dtype)
    part = _lane_partial_sums(x1 * x1)

    @pl.when(pl.program_id(1) == 0)
    def _():
        ss_ref[...] = part

    @pl.when(pl.program_id(1) != 0)
    def _():
        ss_ref[...] = ss_ref[...] + part


def _outproj(a1, a2, w, resid, lnw, tm, tn):
    T, K1 = a1.shape
    K2 = a2.shape[1]
    N = w.shape[1]
    assert K1 == K2 and w.shape[0] == K1 + K2
    return pl.pallas_call(
        _outproj_body,
        grid=(T // tm, N // tn),
        in_specs=[
            pl.BlockSpec((tm, K1), lambda i, j: (i, 0)),
            pl.BlockSpec((tm, K2), lambda i, j: (i, 0)),
            pl.BlockSpec((K1, tn), lambda i, j: (0, j)),
            pl.BlockSpec((K2, tn), lambda i, j: (1, j)),
            pl.BlockSpec((tm, tn), lambda i, j: (i, j)),
            pl.BlockSpec((1, tn), lambda i, j: (0, j)),
        ],
        out_specs=[
            pl.BlockSpec((tm, tn), lambda i, j: (i, j)),
            pl.BlockSpec((tm, tn), lambda i, j: (i, j)),
            pl.BlockSpec((tm, LANES), lambda i, j: (i, 0)),
        ],
        out_shape=[
            jax.ShapeDtypeStruct((T, N), F32),
            jax.ShapeDtypeStruct((T, N), BF16),
            jax.ShapeDtypeStruct((T, LANES), F32),
        ],
        compiler_params=_cparams("parallel", "arbitrary"),
        name="outproj",
    )(a1, a2, w, w, resid, lnw)


def _ffn_up_body(xg_ref, ss_ref, wg_ref, wu_ref, o_ref):
    r = _row_factor(ss_ref, xg_ref.shape[1])
    xg = xg_ref[...]
    g = jnp.dot(xg, wg_ref[...], preferred_element_type=F32) * r
    u = jnp.dot(xg, wu_ref[...], preferred_element_type=F32) * r
    o_ref[...] = (_silu(g) * u).astype(o_ref.dtype)


def _ffn_up(xg, ss, wg, wu, tm, tn):
    T, D = xg.shape
    N = wg.shape[1]
    return pl.pallas_call(
        _ffn_up_body,
        grid=(T // tm, N // tn),
        in_specs=[
            pl.BlockSpec((tm, D), lambda i, j: (i, 0)),
            pl.BlockSpec((tm, LANES), lambda i, j: (i, 0)),
            pl.BlockSpec((D, tn), lambda i, j: (0, j)),
            pl.BlockSpec((D, tn), lambda i, j: (0, j)),
        ],
        out_specs=pl.BlockSpec((tm, tn), lambda i, j: (i, j)),
        out_shape=jax.ShapeDtypeStruct((T, N), BF16),
        compiler_params=_cparams("parallel", "parallel"),
        name="ffn_up",
    )(xg, ss, wg, wu)


def _ffn_down_body(a_ref, w_ref, r_ref, o_ref):
    d = jnp.dot(a_ref[...], w_ref[...], preferred_element_type=F32)

    @pl.when(pl.program_id(2) == 0)
    def _():
        o_ref[...] = r_ref[...] + d

    @pl.when(pl.program_id(2) != 0)
    def _():
        o_ref[...] = o_ref[...] + d


def _ffn_down(a, w, resid, tm, tn, tk):
    T, K = a.shape
    N = w.shape[1]
    return pl.pallas_call(
        _ffn_down_body,
        grid=(T // tm, N // tn, K // tk),
        in_specs=[
            pl.BlockSpec((tm, tk), lambda i, j, k: (i, k)),
            pl.BlockSpec((tk, tn), lambda i, j, k: (k, j)),
            pl.BlockSpec((tm, tn), lambda i, j, k: (i, j)),
        ],
        out_specs=pl.BlockSpec((tm, tn), lambda i, j, k: (i, j)),
        out_shape=jax.ShapeDtypeStruct((T, N), F32),
        compiler_params=_cparams("parallel", "parallel", "arbitrary"),
        name="ffn_down",
    )(a, w, resid)


def _rmsnorm_body(x_ref, w_ref, o_ref):
    x = x_ref[...]
    ms = jnp.mean(x * x, axis=-1, keepdims=True)
    o_ref[...] = x * lax.rsqrt(ms + EPS) * w_ref[...]


def _rmsnorm(x2d, w, tm):
    T, D = x2d.shape
    return pl.pallas_call(
        _rmsnorm_body,
        grid=(T // tm,),
        in_specs=[pl.BlockSpec((tm, D), lambda i: (i, 0)), pl.BlockSpec((1, D), lambda i: (0, 0))],
        out_specs=pl.BlockSpec((tm, D), lambda i: (i, 0)),
        out_shape=jax.ShapeDtypeStruct((T, D), F32),
        compiler_params=_cparams("parallel"),
        name="final_norm",
    )(x2d, w)


def _layer(x, l, ln_mix_w, w_in, conv_w, a_log, dt_bias, dn_norm_w, lambda_q1, lambda_k1, lambda_q2,
           lambda_k2, df_norm_w, w_out, ln_ffn_w, w_gate, w_up, w_down, cos_t, sin_t):
    B, S, D = x.shape
    T = B * S
    x2d = x.reshape(T, D)

    n_a = 4 * DN_WIDTH
    n_ba = 2 * DN_HEADS
    w_in_bf = w_in.astype(BF16)
    w_att = w_in_bf[:, n_a + n_ba:]
    w_ba = jnp.pad(w_in_bf[:, n_a:n_a + n_ba], ((0, 0), (0, LANES - n_ba)))

    tm_in = min(1024, S)
    h = _rms_bf16(x2d, ln_mix_w.reshape(1, D), tm=min(256, T))
    proj_dn, ba = _inproj_dn(h, w_in_bf, w_ba, n_cols=n_a, tm=tm_in, tn=1024)
    qkv_att = _inproj_att(h, w_att, cos_t, sin_t, tm=tm_in, tn=1024).reshape(B, S, -1)
    proj3 = proj_dn.reshape(B, S, -1)

    qkv_dn = _dn_prep(proj3, conv_w, ts=min(512, S), ct=1024)
    n_sc = S // GDN_CHUNK
    ba4 = ba.reshape(B, n_sc, GDN_CHUNK, LANES)
    db_c = ba4[..., :DN_HEADS].transpose(0, 3, 2, 1)
    da_c = ba4[..., DN_HEADS:n_ba].transpose(0, 3, 2, 1)
    alog_b = jnp.broadcast_to(a_log.astype(F32)[:, None, None], (DN_HEADS, 1, n_sc))
    dtb_b = jnp.broadcast_to(dt_bias.astype(F32)[:, None, None], (DN_HEADS, 1, n_sc))
    o_dn = _gdn(alog_b, dtb_b, da_c, db_c, qkv_dn, proj3, dn_norm_w.reshape(1, HEAD_DIM),
                ts=min(GDN_TILE, S), n_heads=GDN_HEADS_PER_STEP, chunk=GDN_CHUNK)

    lambda_init = 0.8 - 0.6 * math.exp(-0.3 * l)
    o_df = _diff_attention(lambda_q1.reshape(1, -1), lambda_k1.reshape(1, -1), lambda_q2.reshape(1, -1),
                           lambda_k2.reshape(1, -1), df_norm_w.reshape(1, -1), qkv_att, lambda_init,
                           tq=min(512, S))

    x1, x1g, x1ss = _outproj(o_dn.reshape(T, -1), o_df.reshape(T, -1), w_out.astype(BF16), x2d,
                             ln_ffn_w.reshape(1, D), tm=min(1024, T), tn=512)

    act = _ffn_up(x1g, x1ss, w_gate.astype(BF16), w_up.astype(BF16), tm=min(FFN_UP_ROWS, T), tn=256)
    x2 = _ffn_down(act, w_down.astype(BF16), x1, tm=min(1024, T), tn=512, tk=w_down.shape[0] // 2)
    return x2.reshape(B, S, D)


def kernel(x, ln_mix_w, w_in, conv_w, a_log, dt_bias, dn_norm_w, lambda_q1, lambda_k1, lambda_q2, lambda_k2,
           df_norm_w, w_out, ln_ffn_w, w_gate, w_up, w_down, ln_final_w):
    B, S, D = x.shape
    cos_t, sin_t = _rope_tables(S)
    depth = w_in.shape[0]
    for l in range(depth):
        x = _layer(x, l, ln_mix_w[l], w_in[l], conv_w[l], a_log[l], dt_bias[l], dn_norm_w[l], lambda_q1[l],
                   lambda_k1[l], lambda_q2[l], lambda_k2[l], df_norm_w[l], w_out[l], ln_ffn_w[l], w_gate[l],
                   w_up[l], w_down[l], cos_t, sin_t)
    out = _rmsnorm(x.reshape(B * S, D), ln_final_w.reshape(1, D), tm=min(256, B * S))
    return out.reshape(B, S, D)
```
